```python
import jax, jax.numpy as jnp
from jax import lax
import numpy as np

D_MODEL = 2048
BATCH = 4
SEQ = 8192
DEPTH = 1
DEC_BATCH = 1
DEC_SEQ = 16384
PAST_LEN = 128

D_CONV = 1024
CONV_WIDTH = 31
N_RET_HEADS = 8
RET_QK_DIM = 128
RET_V_DIM = 256
D_RET_QK = N_RET_HEADS * RET_QK_DIM
D_RET_V = N_RET_HEADS * RET_V_DIM
CHUNK = 128
D_FF = 5632
ROPE_BASE = 10000.0
EPS = 1e-6
N_BRANCH = 2
IN_SIZES = [D_CONV, D_CONV, D_RET_QK, D_RET_QK, D_RET_V, D_RET_V, D_MODEL, D_MODEL]
D_IN = sum(IN_SIZES)
IN_SPLITS = [int(s) for s in np.cumsum(IN_SIZES)[:-1]]

kernel_name = "bidir_conformer_retention_hybrid"


def rms_norm(x, w):
    xf = x.astype(jnp.float32)
    y = xf * lax.rsqrt(jnp.mean(xf * xf, axis=-1, keepdims=True) + EPS)
    return (y * w.astype(jnp.float32)).astype(x.dtype)


def swiglu(x, w1, w3, w2):
    return (jax.nn.silu(x @ w1) * (x @ w3)) @ w2


def conv_branch(a, b, dw_w, dw_b, ln_w, ln_b, w_o):
    u = a * jax.nn.sigmoid(b)
    u = lax.conv_general_dilated(
        u, dw_w[:, None, :], window_strides=(1,),
        padding=[(CONV_WIDTH // 2, CONV_WIDTH // 2)],
        dimension_numbers=('NWC', 'WIO', 'NWC'),
        feature_group_count=D_CONV) + dw_b
    uf = u.astype(jnp.float32)
    mu = jnp.mean(uf, axis=-1, keepdims=True)
    var = jnp.mean(jnp.square(uf - mu), axis=-1, keepdims=True)
    uf = (uf - mu) * lax.rsqrt(var + EPS) * ln_w.astype(jnp.float32) + ln_b.astype(jnp.float32)
    u = jax.nn.silu(uf).astype(a.dtype)
    return u @ w_o


def rotary(x):
    S, d = x.shape[1], x.shape[-1]
    half = d // 2
    inv = ROPE_BASE ** (-jnp.arange(half, dtype=jnp.float32) / half)
    ang = jnp.arange(S, dtype=jnp.float32)[:, None] * inv[None, :]
    cos = jnp.cos(ang)[None, :, None, :]
    sin = jnp.sin(ang)[None, :, None, :]
    x1, x2 = x[..., :half], x[..., half:]
    return jnp.concatenate([x1 * cos - x2 * sin, x1 * sin + x2 * cos], axis=-1)


def retention_chunkwise(q, k, v, log_gamma, include_diag):
    Bn, S, H, dk = q.shape
    dv = v.shape[-1]
    n = S // CHUNK

    def chunks(t):
        return t.reshape(Bn, n, CHUNK, H, t.shape[-1]).transpose(1, 0, 3, 2, 4)

    idx = jnp.arange(CHUNK, dtype=jnp.float32)
    diff = idx[:, None] - idx[None, :]
    mask = (diff >= 0) if include_diag else (diff > 0)
    decay = jnp.where(mask[None], jnp.exp(log_gamma[:, None, None] * jnp.maximum(diff, 0.0)[None]), 0.0)
    xi = jnp.exp(log_gamma[:, None] * (idx + 1.0)[None])
    zeta = jnp.exp(log_gamma[:, None] * (CHUNK - 1.0 - idx)[None])
    chunk_decay = jnp.exp(log_gamma * CHUNK)

    def step(state, qkv):
        qc, kc, vc = qkv
        s = jnp.einsum('bhid,bhjd->bhij', qc, kc) * decay
        o = (jnp.einsum('bhij,bhje->bhie', s, vc)
             + jnp.einsum('bhid,bhde->bhie', qc * xi[..., None], state))
        state = (state * chunk_decay[:, None, None]
                 + jnp.einsum('bhjd,bhje->bhde', kc * zeta[..., None], vc))
        return state, o

    init = jnp.zeros((Bn, H, dk, dv), jnp.float32)
    _, o = lax.scan(step, init, (chunks(q), chunks(k), chunks(v)))
    return o.transpose(1, 0, 3, 2, 4).reshape(Bn, S, H, dv)


def retention_branch(q, k, v, g, decay_fwd, decay_bwd, gn_w, w_o):
    Bn, S, _ = q.shape
    qh = rotary(q.astype(jnp.float32).reshape(Bn, S, N_RET_HEADS, RET_QK_DIM))
    kh = rotary(k.astype(jnp.float32).reshape(Bn, S, N_RET_HEADS, RET_QK_DIM)) * (RET_QK_DIM ** -0.5)
    vh = v.astype(jnp.float32).reshape(Bn, S, N_RET_HEADS, RET_V_DIM)
    lg_f = jax.nn.log_sigmoid(decay_fwd.astype(jnp.float32))
    lg_b = jax.nn.log_sigmoid(decay_bwd.astype(jnp.float32))
    fwd = retention_chunkwise(qh, kh, vh, lg_f, True)
    bwd = retention_chunkwise(qh[:, ::-1], kh[:, ::-1], vh[:, ::-1], lg_b, False)[:, ::-1]
    r = fwd + bwd
    mu = jnp.mean(r, axis=-1, keepdims=True)
    var = jnp.mean(jnp.square(r - mu), axis=-1, keepdims=True)
    r = ((r - mu) * lax.rsqrt(var + EPS)).reshape(Bn, S, D_RET_V) * gn_w.astype(jnp.float32)
    out = (jax.nn.silu(g.astype(jnp.float32)) * r).astype(q.dtype)
    return out @ w_o


def encoder_layer(x, ffn1_norm, ffn1_w1, ffn1_w3, ffn1_w2, mix_norm, w_in, dw_w, dw_b,
                  conv_ln_w, conv_ln_b, w_conv_o, decay_fwd, decay_bwd, ret_gn_w, w_ret_o,
                  w_out, ffn2_norm, ffn2_w1, ffn2_w3, ffn2_w2):
    x = x + 0.5 * swiglu(rms_norm(x, ffn1_norm), ffn1_w1, ffn1_w3, ffn1_w2)
    h = rms_norm(x, mix_norm)
    proj = h @ w_in
    c_a, c_b, q, k, v, g, gate_c, gate_r = jnp.split(proj, IN_SPLITS, axis=-1)
    y_conv = conv_branch(c_a, c_b, dw_w, dw_b, conv_ln_w, conv_ln_b, w_conv_o)
    y_ret = retention_branch(q, k, v, g, decay_fwd, decay_bwd, ret_gn_w, w_ret_o)
    mixed = jax.nn.sigmoid(gate_c) * y_conv + jax.nn.sigmoid(gate_r) * y_ret
    x = x + mixed @ w_out
    x = x + 0.5 * swiglu(rms_norm(x, ffn2_norm), ffn2_w1, ffn2_w3, ffn2_w2)
    return x


def trunk(x, ffn1_norm, ffn1_w1, ffn1_w3, ffn1_w2, mix_norm, w_in, dw_w, dw_b,
          conv_ln_w, conv_ln_b, w_conv_o, decay_fwd, decay_bwd, ret_gn_w, w_ret_o,
          w_out, ffn2_norm, ffn2_w1, ffn2_w3, ffn2_w2, final_norm):
    for l in range(DEPTH):
        x = encoder_layer(x, ffn1_norm[l], ffn1_w1[l], ffn1_w3[l], ffn1_w2[l], mix_norm[l], w_in[l],
                          dw_w[l], dw_b[l], conv_ln_w[l], conv_ln_b[l], w_conv_o[l], decay_fwd[l],
                          decay_bwd[l], ret_gn_w[l], w_ret_o[l], w_out[l], ffn2_norm[l],
                          ffn2_w1[l], ffn2_w3[l], ffn2_w2[l])
    return rms_norm(x, final_norm)


def setup_inputs(seed: int = 0) -> dict:
    key = jax.random.key(seed)
    ks = jax.random.split(key, 32)
    f32 = jnp.float32

    def w(k, shape, fan_in):
        return jax.random.normal(k, shape, f32) * (fan_in ** -0.5)

    def gain(k, shape):
        return 1.0 + 0.02 * jax.random.normal(k, shape, f32)

    gamma0 = 1.0 - 2.0 ** (-5.0 - np.arange(N_RET_HEADS))
    logit0 = jnp.asarray(np.log(gamma0 / (1.0 - gamma0)), dtype=f32)
    return {
        "x_prompt": jax.random.normal(ks[0], (BATCH, SEQ, D_MODEL), f32),
        "x_sample": jax.random.normal(ks[1], (DEC_BATCH, DEC_SEQ, D_MODEL), f32),
        "ffn1_norm": gain(ks[2], (DEPTH, D_MODEL)),
        "ffn1_w1": w(ks[3], (DEPTH, D_MODEL, D_FF), D_MODEL),
        "ffn1_w3": w(ks[4], (DEPTH, D_MODEL, D_FF), D_MODEL),
        "ffn1_w2": w(ks[5], (DEPTH, D_FF, D_MODEL), D_FF),
        "mix_norm": gain(ks[6], (DEPTH, D_MODEL)),
        "w_in": w(ks[7], (DEPTH, D_MODEL, D_IN), D_MODEL),
        "dw_w": w(ks[8], (DEPTH, CONV_WIDTH, D_CONV), CONV_WIDTH),
        "dw_b": 0.02 * jax.random.normal(ks[9], (DEPTH, D_CONV), f32),
        "conv_ln_w": gain(ks[10], (DEPTH, D_CONV)),
        "conv_ln_b": 0.02 * jax.random.normal(ks[11], (DEPTH, D_CONV), f32),
        "w_conv_o": w(ks[12], (DEPTH, D_CONV, D_MODEL), D_CONV),
        "decay_fwd": logit0[None, :] + 0.1 * jax.random.normal(ks[13], (DEPTH, N_RET_HEADS), f32),
        "decay_bwd": logit0[None, :] + 0.1 * jax.random.normal(ks[14], (DEPTH, N_RET_HEADS), f32),
        "ret_gn_w": gain(ks[15], (DEPTH, D_RET_V)),
        "w_ret_o": w(ks[16], (DEPTH, D_RET_V, D_MODEL), D_RET_V),
        "w_out": w(ks[17], (DEPTH, D_MODEL, D_MODEL), D_MODEL),
        "ffn2_norm": gain(ks[18], (DEPTH, D_MODEL)),
        "ffn2_w1": w(ks[19], (DEPTH, D_MODEL, D_FF), D_MODEL),
        "ffn2_w3": w(ks[20], (DEPTH, D_MODEL, D_FF), D_MODEL),
        "ffn2_w2": w(ks[21], (DEPTH, D_FF, D_MODEL), D_FF),
        "final_norm": gain(ks[22], (D_MODEL,)),
    }


def reference(x_prompt, x_sample, ffn1_norm, ffn1_w1, ffn1_w3, ffn1_w2, mix_norm, w_in, dw_w, dw_b,
              conv_ln_w, conv_ln_b, w_conv_o, decay_fwd, decay_bwd, ret_gn_w, w_ret_o, w_out,
              ffn2_norm, ffn2_w1, ffn2_w3, ffn2_w2, final_norm):
    y_prompt = trunk(x_prompt, ffn1_norm, ffn1_w1, ffn1_w3, ffn1_w2, mix_norm, w_in, dw_w, dw_b,
                     conv_ln_w, conv_ln_b, w_conv_o, decay_fwd, decay_bwd, ret_gn_w, w_ret_o, w_out,
                     ffn2_norm, ffn2_w1, ffn2_w3, ffn2_w2, final_norm)
    y_sample = trunk(x_sample, ffn1_norm, ffn1_w1, ffn1_w3, ffn1_w2, mix_norm, w_in, dw_w, dw_b,
                     conv_ln_w, conv_ln_b, w_conv_o, decay_fwd, decay_bwd, ret_gn_w, w_ret_o, w_out,
                     ffn2_norm, ffn2_w1, ffn2_w3, ffn2_w2, final_norm)
    return (y_prompt, y_sample)
```

```python
import functools

import jax
import jax.numpy as jnp
from jax import lax
from jax.experimental import pallas as pl
from jax.experimental.pallas import tpu as pltpu

D_MODEL = 2048
D_CONV = 1024
CONV_WIDTH = 31
N_HEADS = 8
QK_DIM = 128
V_DIM = 256
D_QK = N_HEADS * QK_DIM
D_V = N_HEADS * V_DIM
CHUNK = 128
D_FF = 5632
ROPE_BASE = 10000.0
EPS = 1e-6

BF16 = jnp.bfloat16
F32 = jnp.float32

VMEM_LIMIT_BYTES = 56 * 1024 * 1024
SUBLANES = 8
LANES = 128

FFN_TM = 512
FFN_TF = 512
PROJ_TM = 1024
PROJ_TN = 1024
GLU_TN = 512
CONV_TS = 128
CONV_HALO = 16
CONV_ROWS = 32
MERGE_TM = 256

COL_Q, COL_K, COL_V, COL_G, COL_GC, COL_GR = 0, 1, 2, 4, 6, 8
N_MAIN = D_QK * 2 + D_V * 2 + D_MODEL * 2


def _params(*sem):
    return pltpu.CompilerParams(dimension_semantics=sem,
                                vmem_limit_bytes=VMEM_LIMIT_BYTES)


def _rms(x, g):
    ms = jnp.mean(x * x, axis=-1, keepdims=True)
    return x * lax.rsqrt(ms + EPS) * g


def _silu(x):
    return x * jax.nn.sigmoid(x)


def _ffn_kernel(x_ref, g_ref, w1_ref, w3_ref, w2_ref, g2_ref, *rest, emit_normed):
    if emit_normed:
        o_ref, h_ref, hn_ref = rest
    else:
        o_ref, hn_ref = rest
    j = pl.program_id(1)

    @pl.when(j == 0)
    def _():
        hn_ref[...] = _rms(x_ref[...], g_ref[...]).astype(BF16)

    hn = hn_ref[...]
    a = jnp.dot(hn, w1_ref[...], preferred_element_type=F32)
    b = jnp.dot(hn, w3_ref[...], preferred_element_type=F32)
    p = (_silu(a) * b).astype(BF16)
    contrib = jnp.dot(p, w2_ref[...], preferred_element_type=F32)

    @pl.when(j == 0)
    def _():
        o_ref[...] = contrib

    @pl.when(j > 0)
    def _():
        o_ref[...] += contrib

    @pl.when(j == pl.num_programs(1) - 1)
    def _():
        y = x_ref[...] + 0.5 * o_ref[...]
        if emit_normed:
            o_ref[...] = y
            h_ref[...] = _rms(y, g2_ref[...]).astype(BF16)
        else:
            o_ref[...] = _rms(y, g2_ref[...])


def _ffn(x, g, w1, w3, w2, g2, *, emit_normed, name):
    t = x.shape[0]
    tm, tf = FFN_TM, FFN_TF
    assert t % tm == 0 and D_FF % tf == 0
    row = pl.BlockSpec((tm, D_MODEL), lambda i, j: (i, 0))
    vec = pl.BlockSpec((1, D_MODEL), lambda i, j: (0, 0))
    out_shape = [jax.ShapeDtypeStruct((t, D_MODEL), F32)]
    out_specs = [row]
    if emit_normed:
        out_shape.append(jax.ShapeDtypeStruct((t, D_MODEL), BF16))
        out_specs.append(row)
    return pl.pallas_call(
        functools.partial(_ffn_kernel, emit_normed=emit_normed),
        grid=(t // tm, D_FF // tf),
        in_specs=[row, vec,
                  pl.BlockSpec((D_MODEL, tf), lambda i, j: (0, j)),
                  pl.BlockSpec((D_MODEL, tf), lambda i, j: (0, j)),
                  pl.BlockSpec((tf, D_MODEL), lambda i, j: (j, 0)),
                  vec],
        out_specs=out_specs,
        out_shape=out_shape,
        scratch_shapes=[pltpu.VMEM((tm, D_MODEL), BF16)],
        compiler_params=_params("parallel", "arbitrary"),
        name=name,
    )(x, g, w1, w3, w2, g2)


def _glu_kernel(h_ref, wa_ref, wb_ref, u_ref):
    h = h_ref[...]
    a = jnp.dot(h, wa_ref[...], preferred_element_type=F32)
    b = jnp.dot(h, wb_ref[...], preferred_element_type=F32)
    u_ref[...] = a * jax.nn.sigmoid(b)


def _glu_proj(h, wa, wb):
    t = h.shape[0]
    tm, tn = PROJ_TM, GLU_TN
    assert t % tm == 0 and D_CONV % tn == 0
    wspec = pl.BlockSpec((D_MODEL, tn), lambda i, j: (0, j))
    return pl.pallas_call(
        _glu_kernel,
        grid=(t // tm, D_CONV // tn),
        in_specs=[pl.BlockSpec((tm, D_MODEL), lambda i, j: (i, 0)), wspec, wspec],
        out_specs=pl.BlockSpec((tm, tn), lambda i, j: (i, j)),
        out_shape=jax.ShapeDtypeStruct((t, D_CONV), F32),
        compiler_params=_params("parallel", "arbitrary"),
        name="glu_proj",
    )(h, wa, wb)


def _main_proj_kernel(h_ref, w_ref, cos_ref, sin_ref, o_ref):
    j = pl.program_id(1)
    y = jnp.dot(h_ref[...], w_ref[...], preferred_element_type=F32)

    @pl.when(j < COL_V)
    def _():
        scale = jnp.where(j == COL_K, QK_DIM ** -0.5, 1.0).astype(F32)
        cos = cos_ref[...]
        sin = sin_ref[...]
        for hd in range(PROJ_TN // QK_DIM):
            sl = slice(hd * QK_DIM, (hd + 1) * QK_DIM)
            xh = y[:, sl]
            rot = xh * cos + pltpu.roll(xh, QK_DIM // 2, 1) * sin
            o_ref[:, sl] = (rot * scale).astype(BF16)

    @pl.when(jnp.logical_and(j >= COL_V, j < COL_G))
    def _():
        o_ref[...] = y.astype(BF16)

    @pl.when(jnp.logical_and(j >= COL_G, j < COL_GC))
    def _():
        o_ref[...] = _silu(y).astype(BF16)

    @pl.when(j >= COL_GC)
    def _():
        o_ref[...] = jax.nn.sigmoid(y).astype(BF16)


def _main_proj(h, w, cosf, sinf, seq):
    t = h.shape[0]
    tm, tn = PROJ_TM, PROJ_TN
    assert t % tm == 0 and seq % tm == 0 and N_MAIN % tn == 0
    tiles_per_seq = seq // tm
    tab = pl.BlockSpec((tm, QK_DIM), lambda i, j: (i % tiles_per_seq, 0))
    return pl.pallas_call(
        _main_proj_kernel,
        grid=(t // tm, N_MAIN // tn),
        in_specs=[pl.BlockSpec((tm, D_MODEL), lambda i, j: (i, 0)),
                  pl.BlockSpec((D_MODEL, tn), lambda i, j: (0, j)),
                  tab, tab],
        out_specs=pl.BlockSpec((tm, tn), lambda i, j: (i, j)),
        out_shape=jax.ShapeDtypeStruct((t, N_MAIN), BF16),
        compiler_params=_params("parallel", "arbitrary"),
        name="main_proj",
    )(h, w, cosf, sinf)


def _conv_kernel(prev_ref, cur_ref, next_ref, w_ref, b_ref, lnw_ref, lnb_ref,
                 o_ref, pad_ref, y_ref, *, tiles_per_seq):
    ts = CONV_TS
    il = pl.program_id(0) % tiles_per_seq
    pad_ref[0:CONV_HALO, :] = jnp.where(il > 0, prev_ref[...], 0.0)
    pad_ref[CONV_HALO:CONV_HALO + ts, :] = cur_ref[...]
    pad_ref[CONV_HALO + ts:, :] = jnp.where(il < tiles_per_seq - 1, next_ref[...], 0.0)

    rows = CONV_ROWS
    for r0 in range(0, ts, rows):
        for c0 in range(0, D_CONV, LANES):
            cs = slice(c0, c0 + LANES)
            acc = None
            for b in range(SUBLANES):
                part = None
                for a in range(CONV_WIDTH // SUBLANES + 1):
                    s = SUBLANES * a + b
                    if s < 1 or s > CONV_WIDTH:
                        continue
                    xs = pad_ref[r0 + SUBLANES * a:r0 + SUBLANES * a + rows + SUBLANES, cs]
                    term = xs * w_ref[s - 1:s, cs]
                    part = term if part is None else part + term
                part = part[b:b + rows, :]
                acc = part if acc is None else acc + part
            y_ref[r0:r0 + rows, cs] = acc

    y = y_ref[...] + b_ref[...]
    mu = jnp.mean(y, axis=-1, keepdims=True)
    d = y - mu
    var = jnp.mean(d * d, axis=-1, keepdims=True)
    z = d * lax.rsqrt(var + EPS) * lnw_ref[...] + lnb_ref[...]
    o_ref[...] = _silu(z).astype(BF16)


def _conv_branch(u, dw_w, dw_b, ln_w, ln_b, seq):
    t = u.shape[0]
    ts = CONV_TS
    assert seq % ts == 0 and ts % CONV_HALO == 0 and CONV_HALO > CONV_WIDTH // 2
    hb = ts // CONV_HALO
    last = t // CONV_HALO - 1
    vec = pl.BlockSpec((1, D_CONV), lambda i: (0, 0))
    return pl.pallas_call(
        functools.partial(_conv_kernel, tiles_per_seq=seq // ts),
        grid=(t // ts,),
        in_specs=[
            pl.BlockSpec((CONV_HALO, D_CONV), lambda i: (jnp.maximum(i * hb - 1, 0), 0)),
            pl.BlockSpec((ts, D_CONV), lambda i: (i, 0)),
            pl.BlockSpec((CONV_HALO, D_CONV), lambda i: (jnp.minimum((i + 1) * hb, last), 0)),
            pl.BlockSpec((CONV_WIDTH, D_CONV), lambda i: (0, 0)),
            vec, vec, vec],
        out_specs=pl.BlockSpec((ts, D_CONV), lambda i: (i, 0)),
        out_shape=jax.ShapeDtypeStruct((t, D_CONV), BF16),
        scratch_shapes=[pltpu.VMEM((ts + 2 * CONV_HALO, D_CONV), F32),
                        pltpu.VMEM((ts, D_CONV), F32)],
        compiler_params=_params("parallel"),
        name="conv_branch",
    )(u, u, u, dw_w, dw_b, ln_w, ln_b)


def _log_sigmoid(x):
    return -(jnp.maximum(-x, 0.0) + jnp.log1p(jnp.exp(-jnp.abs(x))))


def _ret_kernel(qf_ref, kf_ref, vf_ref, qb_ref, kb_ref, vb_ref, df_ref, db_ref,
                of_ref, ob_ref, dmat_ref, xi_ref, zeta_ref, state_ref):
    c = pl.program_id(1)
    ch = CHUNK

    @pl.when(c == 0)
    def _():
        state_ref[...] = jnp.zeros_like(state_ref)
        row = lax.broadcasted_iota(jnp.int32, (ch, ch), 0).astype(F32)
        col = lax.broadcasted_iota(jnp.int32, (ch, ch), 1).astype(F32)
        for h in range(N_HEADS):
            lg_f = _log_sigmoid(df_ref[h])[:, :ch]
            lg_b = _log_sigmoid(db_ref[h])[:, :ch]
            diff_f = row - col
            diff_b = col - row
            dmat_ref[0, h] = jnp.where(diff_f >= 0, jnp.exp(lg_f * jnp.maximum(diff_f, 0.0)), 0.0)
            dmat_ref[1, h] = jnp.where(diff_b > 0, jnp.exp(lg_b * jnp.maximum(diff_b, 0.0)), 0.0)
            xi_ref[0, h] = jnp.exp(lg_f * (row + 1.0))
            xi_ref[1, h] = jnp.exp(lg_b * (ch - row))
            zeta_ref[0, h] = jnp.exp(lg_f * (ch - 1.0 - row))
            zeta_ref[1, h] = jnp.exp(lg_b * row)

    def one(d, h, q_ref, k_ref, v_ref, o_ref, dec_ref):
        qs = slice(h * QK_DIM, (h + 1) * QK_DIM)
        vs = slice(h * V_DIM, (h + 1) * V_DIM)
        q = q_ref[:, qs]
        k = k_ref[:, qs]
        v = v_ref[:, vs]
        s = lax.dot_general(q, k, (((1,), (1,)), ((), ())), preferred_element_type=F32)
        s = (s * dmat_ref[d, h]).astype(BF16)
        qx = (q.astype(F32) * xi_ref[d, h]).astype(BF16)
        state = state_ref[d, h]
        o = (jnp.dot(s, v, preferred_element_type=F32)
             + jnp.dot(qx, state.astype(BF16), preferred_element_type=F32))
        o_ref[:, vs] = o
        kz = (k.astype(F32) * zeta_ref[d, h]).astype(BF16)
        upd = lax.dot_general(kz, v, (((0,), (0,)), ((), ())), preferred_element_type=F32)
        cd = jnp.exp(_log_sigmoid(dec_ref[h]) * float(ch))
        state_ref[d, h] = state * cd + upd

    for h in range(N_HEADS):
        one(0, h, qf_ref, kf_ref, vf_ref, of_ref, df_ref)
        one(1, h, qb_ref, kb_ref, vb_ref, ob_ref, db_ref)


def _retention(p, dec_f, dec_b, nseq, seq):
    t = p.shape[0]
    ch = CHUNK
    n = seq // ch
    assert seq % ch == 0 and t == nseq * seq

    def fwd(colblk):
        return lambda b, c: (b * n + c, colblk)

    def bwd(colblk):
        return lambda b, c: (b * n + (n - 1 - c), colblk)

    qk = lambda f, blk: pl.BlockSpec((ch, D_QK), f(blk))
    vv = lambda f: pl.BlockSpec((ch, D_V), f(COL_V * PROJ_TN // D_V))
    dec = pl.BlockSpec((N_HEADS, 1, V_DIM), lambda b, c: (0, 0, 0))
    q_blk, k_blk = COL_Q * PROJ_TN // D_QK, COL_K * PROJ_TN // D_QK
    out = jax.ShapeDtypeStruct((t, D_V), F32)
    return pl.pallas_call(
        _ret_kernel,
        grid=(nseq, n),
        in_specs=[qk(fwd, q_blk), qk(fwd, k_blk), vv(fwd),
                  qk(bwd, q_blk), qk(bwd, k_blk), vv(bwd), dec, dec],
        out_specs=[pl.BlockSpec((ch, D_V), fwd(0)), pl.BlockSpec((ch, D_V), bwd(0))],
        out_shape=[out, out],
        scratch_shapes=[pltpu.VMEM((2, N_HEADS, ch, ch), F32),
                        pltpu.VMEM((2, N_HEADS, ch, QK_DIM), F32),
                        pltpu.VMEM((2, N_HEADS, ch, QK_DIM), F32),
                        pltpu.VMEM((2, N_HEADS, QK_DIM, V_DIM), F32)],
        compiler_params=_params("arbitrary", "arbitrary"),
        name="retention",
    )(p, p, p, p, p, p, dec_f, dec_b)


def _merge_kernel(of_ref, ob_ref, g_ref, gc_ref, gr_ref, uc_ref, x_ref, gn_ref,
                  wr_ref, wc_ref, wo_ref, o_ref, rg_ref):
    for h in range(N_HEADS):
        vs = slice(h * V_DIM, (h + 1) * V_DIM)
        r = of_ref[:, vs] + ob_ref[:, vs]
        mu = jnp.mean(r, axis=-1, keepdims=True)
        d = r - mu
        var = jnp.mean(d * d, axis=-1, keepdims=True)
        rn = d * lax.rsqrt(var + EPS) * gn_ref[:, vs]
        rg_ref[:, vs] = (g_ref[:, vs].astype(F32) * rn).astype(BF16)
    y_ret = jnp.dot(rg_ref[...], wr_ref[...], preferred_element_type=F32)
    y_conv = jnp.dot(uc_ref[...], wc_ref[...], preferred_element_type=F32)
    mixed = gc_ref[...].astype(F32) * y_conv + gr_ref[...].astype(F32) * y_ret
    o_ref[...] = x_ref[...] + jnp.dot(mixed.astype(BF16), wo_ref[...],
                                      preferred_element_type=F32)


def _merge(o_f, o_b, p, uc, x1, gn_w, w_ret_o, w_conv_o, w_out):
    t = x1.shape[0]
    tm = MERGE_TM
    assert t % tm == 0
    row = lambda width, blk=0: pl.BlockSpec((tm, width), lambda i: (i, blk))
    full = lambda a: pl.BlockSpec(a.shape, lambda i: (0, 0))
    pblk = lambda col: row(D_MODEL, col * PROJ_TN // D_MODEL)
    return pl.pallas_call(
        _merge_kernel,
        grid=(t // tm,),
        in_specs=[row(D_V), row(D_V), pblk(COL_G), pblk(COL_GC), pblk(COL_GR),
                  row(D_CONV), row(D_MODEL), full(gn_w),
                  full(w_ret_o), full(w_conv_o), full(w_out)],
        out_specs=row(D_MODEL),
        out_shape=jax.ShapeDtypeStruct((t, D_MODEL), F32),
        scratch_shapes=[pltpu.VMEM((tm, D_V), BF16)],
        compiler_params=_params("parallel"),
        name="merge",
    )(o_f, o_b, p, p, p, uc, x1, gn_w, w_ret_o, w_conv_o, w_out)


def _rope_tables(seq):
    half = QK_DIM // 2
    inv = ROPE_BASE ** (-jnp.arange(half, dtype=F32) / half)
    ang = jnp.arange(seq, dtype=F32)[:, None] * inv[None, :]
    cos, sin = jnp.cos(ang), jnp.sin(ang)
    return (jnp.concatenate([cos, cos], axis=-1),
            jnp.concatenate([-sin, sin], axis=-1))


def _trunk(x, wts):
    nseq, seq, _ = x.shape
    xf = x.reshape(nseq * seq, D_MODEL)
    x1, h = _ffn(xf, wts["ffn1_norm"], wts["ffn1_w1"], wts["ffn1_w3"], wts["ffn1_w2"],
                 wts["mix_norm"], emit_normed=True, name="ffn1")
    u = _glu_proj(h, wts["w_a"], wts["w_b"])
    cosf, sinf = _rope_tables(seq)
    p = _main_proj(h, wts["w_main"], cosf, sinf, seq)
    uc = _conv_branch(u, wts["dw_w"], wts["dw_b"], wts["conv_ln_w"], wts["conv_ln_b"], seq)
    o_f, o_b = _retention(p, wts["dec_f"], wts["dec_b"], nseq, seq)
    x2 = _merge(o_f, o_b, p, uc, x1, wts["ret_gn_w"], wts["w_ret_o"], wts["w_conv_o"],
                wts["w_out"])
    (y,) = _ffn(x2, wts["ffn2_norm"], wts["ffn2_w1"], wts["ffn2_w3"], wts["ffn2_w2"],
                wts["final_norm"], emit_normed=False, name="ffn2")
    return y.reshape(nseq, seq, D_MODEL)


def kernel(x_prompt, x_sample, ffn1_norm, ffn1_w1, ffn1_w3, ffn1_w2, mix_norm, w_in, dw_w, dw_b,
           conv_ln_w, conv_ln_b, w_conv_o, decay_fwd, decay_bwd, ret_gn_w, w_ret_o, w_out,
           ffn2_norm, ffn2_w1, ffn2_w3, ffn2_w2, final_norm):
    assert ffn1_w1.shape[0] == 1, "single layer"
    bf = lambda a: a.astype(BF16)
    vec = lambda a: a.reshape(1, -1).astype(F32)
    w_in0 = w_in[0]
    dec = lambda a: jnp.broadcast_to(a[0].astype(F32)[:, None, None], (N_HEADS, 1, V_DIM))
    wts = dict(
        ffn1_norm=vec(ffn1_norm[0]), ffn1_w1=bf(ffn1_w1[0]), ffn1_w3=bf(ffn1_w3[0]),
        ffn1_w2=bf(ffn1_w2[0]), mix_norm=vec(mix_norm[0]),
        w_a=bf(w_in0[:, :D_CONV]), w_b=bf(w_in0[:, D_CONV:2 * D_CONV]),
        w_main=bf(w_in0[:, 2 * D_CONV:]),
        dw_w=dw_w[0].astype(F32), dw_b=vec(dw_b[0]),
        conv_ln_w=vec(conv_ln_w[0]), conv_ln_b=vec(conv_ln_b[0]),
        w_conv_o=bf(w_conv_o[0]), dec_f=dec(decay_fwd), dec_b=dec(decay_bwd),
        ret_gn_w=vec(ret_gn_w[0]), w_ret_o=bf(w_ret_o[0]), w_out=bf(w_out[0]),
        ffn2_norm=vec(ffn2_norm[0]), ffn2_w1=bf(ffn2_w1[0]), ffn2_w3=bf(ffn2_w3[0]),
        ffn2_w2=bf(ffn2_w2[0]), final_norm=vec(final_norm),
    )
    return (_trunk(x_prompt, wts), _trunk(x_sample, wts))
```

```python
import functools

import jax
import jax.numpy as jnp
from jax import lax
from jax.experimental import pallas as pl
from jax.experimental.pallas import tpu as pltpu

D_MODEL = 2048
D_CONV = 1024
CONV_WIDTH = 31
N_HEADS = 8
QK_DIM = 128
V_DIM = 256
D_QK = N_HEADS * QK_DIM
D_V = N_HEADS * V_DIM
CHUNK = 128
D_FF = 5632
ROPE_BASE = 10000.0
EPS = 1e-6

BF16 = jnp.bfloat16
F32 = jnp.float32

VMEM_LIMIT_BYTES = 56 * 1024 * 1024
SUBLANES = 8
LANES = 128

FFN_TM = 512
FFN_TF = 512
PROJ_TM = 1024
PROJ_TN = 1024
GLU_TN = 512
CONV_TS = 128
CONV_HALO = 16
CONV_ROWS = 32
MERGE_TM = 256

VG_G, VG_GC = D_V // PROJ_TN, 2 * D_V // PROJ_TN
BLK_V, BLK_G, BLK_GC, BLK_GR = 0, 1, 2, 3


def _params(*sem):
    return pltpu.CompilerParams(dimension_semantics=sem,
                                vmem_limit_bytes=VMEM_LIMIT_BYTES)


def _rms(x, g):
    ms = jnp.mean(x * x, axis=-1, keepdims=True)
    return x * lax.rsqrt(ms + EPS) * g


def _silu(x):
    return x * jax.nn.sigmoid(x)


def _ffn_kernel(x_ref, g_ref, w1_ref, w3_ref, w2_ref, g2_ref, *rest, emit_normed):
    if emit_normed:
        o_ref, h_ref, hn_ref = rest
    else:
        o_ref, hn_ref = rest
    j = pl.program_id(1)

    @pl.when(j == 0)
    def _():
        hn_ref[...] = _rms(x_ref[...], g_ref[...]).astype(BF16)
        o_ref[...] = jnp.zeros_like(o_ref)

    hn = hn_ref[...]
    a = jnp.dot(hn, w1_ref[...], preferred_element_type=F32)
    b = jnp.dot(hn, w3_ref[...], preferred_element_type=F32)
    p = (_silu(a) * b).astype(BF16)
    o_ref[...] += jnp.dot(p, w2_ref[...], preferred_element_type=F32)

    @pl.when(j == pl.num_programs(1) - 1)
    def _():
        y = x_ref[...] + 0.5 * o_ref[...]
        if emit_normed:
            o_ref[...] = y
            h_ref[...] = _rms(y, g2_ref[...]).astype(BF16)
        else:
            o_ref[...] = _rms(y, g2_ref[...])


def _ffn(x, g, w1, w3, w2, g2, *, emit_normed, name):
    t = x.shape[0]
    tm, tf = FFN_TM, FFN_TF
    assert t % tm == 0 and D_FF % tf == 0
    row = pl.BlockSpec((tm, D_MODEL), lambda i, j: (i, 0))
    vec = pl.BlockSpec((1, D_MODEL), lambda i, j: (0, 0))
    out_shape = [jax.ShapeDtypeStruct((t, D_MODEL), F32)]
    out_specs = [row]
    if emit_normed:
        out_shape.append(jax.ShapeDtypeStruct((t, D_MODEL), BF16))
        out_specs.append(row)
    return pl.pallas_call(
        functools.partial(_ffn_kernel, emit_normed=emit_normed),
        grid=(t // tm, D_FF // tf),
        in_specs=[row, vec,
                  pl.BlockSpec((D_MODEL, tf), lambda i, j: (0, j)),
                  pl.BlockSpec((D_MODEL, tf), lambda i, j: (0, j)),
                  pl.BlockSpec((tf, D_MODEL), lambda i, j: (j, 0)),
                  vec],
        out_specs=out_specs,
        out_shape=out_shape,
        scratch_shapes=[pltpu.VMEM((tm, D_MODEL), BF16)],
        compiler_params=_params("parallel", "arbitrary"),
        name=name,
    )(x, g, w1, w3, w2, g2)


def _glu_kernel(h_ref, wa_ref, wb_ref, u_ref):
    h = h_ref[...]
    a = jnp.dot(h, wa_ref[...], preferred_element_type=F32)
    b = jnp.dot(h, wb_ref[...], preferred_element_type=F32)
    u_ref[...] = a * jax.nn.sigmoid(b)


def _glu_proj(h, wa, wb):
    t = h.shape[0]
    tm, tn = PROJ_TM, GLU_TN
    assert t % tm == 0 and D_CONV % tn == 0
    wspec = pl.BlockSpec((D_MODEL, tn), lambda i, j: (0, j))
    return pl.pallas_call(
        _glu_kernel,
        grid=(t // tm, D_CONV // tn),
        in_specs=[pl.BlockSpec((tm, D_MODEL), lambda i, j: (i, 0)), wspec, wspec],
        out_specs=pl.BlockSpec((tm, tn), lambda i, j: (i, j)),
        out_shape=jax.ShapeDtypeStruct((t, D_CONV), F32),
        compiler_params=_params("parallel", "arbitrary"),
        name="glu_proj",
    )(h, wa, wb)


def _qk_proj_kernel(h_ref, w_ref, cos_ref, sin_ref, o_ref):
    scale = jnp.where(pl.program_id(1) == 1, QK_DIM ** -0.5, 1.0).astype(F32)
    y = jnp.dot(h_ref[...], w_ref[...], preferred_element_type=F32)
    cos = cos_ref[...]
    sin = sin_ref[...]
    for hd in range(D_QK // QK_DIM):
        sl = slice(hd * QK_DIM, (hd + 1) * QK_DIM)
        xh = y[:, sl]
        rot = xh * cos + pltpu.roll(xh, QK_DIM // 2, 1) * sin
        o_ref[:, sl] = (rot * scale).astype(BF16)


def _vg_proj_kernel(h_ref, w_ref, o_ref):
    j = pl.program_id(1)
    y = jnp.dot(h_ref[...], w_ref[...], preferred_element_type=F32)
    sig = jax.nn.sigmoid(y)
    out = jnp.where(j < VG_G, y, jnp.where(j < VG_GC, y * sig, sig))
    o_ref[...] = out.astype(BF16)


def _proj(kernel_fn, h, w, extra, extra_specs, name):
    t = h.shape[0]
    n = w.shape[1]
    tm, tn = PROJ_TM, PROJ_TN
    assert t % tm == 0 and n % tn == 0
    return pl.pallas_call(
        kernel_fn,
        grid=(t // tm, n // tn),
        in_specs=[pl.BlockSpec((tm, D_MODEL), lambda i, j: (i, 0)),
                  pl.BlockSpec((D_MODEL, tn), lambda i, j: (0, j))] + extra_specs,
        out_specs=pl.BlockSpec((tm, tn), lambda i, j: (i, j)),
        out_shape=jax.ShapeDtypeStruct((t, n), BF16),
        compiler_params=_params("parallel", "arbitrary"),
        name=name,
    )(h, w, *extra)


def _qk_proj(h, w, cosf, sinf, seq):
    assert seq % PROJ_TM == 0 and PROJ_TN == D_QK
    tiles_per_seq = seq // PROJ_TM
    tab = pl.BlockSpec((PROJ_TM, QK_DIM), lambda i, j: (i % tiles_per_seq, 0))
    return _proj(_qk_proj_kernel, h, w, [cosf, sinf], [tab, tab], "qk_proj")


def _vg_proj(h, w):
    return _proj(_vg_proj_kernel, h, w, [], [], "vg_proj")


def _conv_kernel(prev_ref, cur_ref, next_ref, w_ref, b_ref, lnw_ref, lnb_ref,
                 o_ref, pad_ref, y_ref, *, tiles_per_seq):
    ts = CONV_TS
    il = pl.program_id(0) % tiles_per_seq
    pad_ref[0:CONV_HALO, :] = jnp.where(il > 0, prev_ref[...], 0.0)
    pad_ref[CONV_HALO:CONV_HALO + ts, :] = cur_ref[...]
    pad_ref[CONV_HALO + ts:, :] = jnp.where(il < tiles_per_seq - 1, next_ref[...], 0.0)

    rows = CONV_ROWS
    for r0 in range(0, ts, rows):
        for c0 in range(0, D_CONV, LANES):
            cs = slice(c0, c0 + LANES)
            acc = None
            for b in range(SUBLANES):
                part = None
                for a in range(CONV_WIDTH // SUBLANES + 1):
                    s = SUBLANES * a + b
                    if s < 1 or s > CONV_WIDTH:
                        continue
                    xs = pad_ref[r0 + SUBLANES * a:r0 + SUBLANES * a + rows + SUBLANES, cs]
                    term = xs * w_ref[s - 1:s, cs]
                    part = term if part is None else part + term
                part = part[b:b + rows, :]
                acc = part if acc is None else acc + part
            y_ref[r0:r0 + rows, cs] = acc

    y = y_ref[...] + b_ref[...]
    mu = jnp.mean(y, axis=-1, keepdims=True)
    d = y - mu
    var = jnp.mean(d * d, axis=-1, keepdims=True)
    z = d * lax.rsqrt(var + EPS) * lnw_ref[...] + lnb_ref[...]
    o_ref[...] = _silu(z).astype(BF16)


def _conv_branch(u, dw_w, dw_b, ln_w, ln_b, seq):
    t = u.shape[0]
    ts = CONV_TS
    assert seq % ts == 0 and ts % CONV_HALO == 0 and CONV_HALO > CONV_WIDTH // 2
    hb = ts // CONV_HALO
    last = t // CONV_HALO - 1
    vec = pl.BlockSpec((1, D_CONV), lambda i: (0, 0))
    return pl.pallas_call(
        functools.partial(_conv_kernel, tiles_per_seq=seq // ts),
        grid=(t // ts,),
        in_specs=[
            pl.BlockSpec((CONV_HALO, D_CONV), lambda i: (jnp.maximum(i * hb - 1, 0), 0)),
            pl.BlockSpec((ts, D_CONV), lambda i: (i, 0)),
            pl.BlockSpec((CONV_HALO, D_CONV), lambda i: (jnp.minimum((i + 1) * hb, last), 0)),
            pl.BlockSpec((CONV_WIDTH, D_CONV), lambda i: (0, 0)),
            vec, vec, vec],
        out_specs=pl.BlockSpec((ts, D_CONV), lambda i: (i, 0)),
        out_shape=jax.ShapeDtypeStruct((t, D_CONV), BF16),
        scratch_shapes=[pltpu.VMEM((ts + 2 * CONV_HALO, D_CONV), F32),
                        pltpu.VMEM((ts, D_CONV), F32)],
        compiler_params=_params("parallel"),
        name="conv_branch",
    )(u, u, u, dw_w, dw_b, ln_w, ln_b)


def _log_sigmoid(x):
    return -(jnp.maximum(-x, 0.0) + jnp.log1p(jnp.exp(-jnp.abs(x))))


def _ret_kernel(qf_ref, kf_ref, vf_ref, qb_ref, kb_ref, vb_ref, df_ref, db_ref,
                of_ref, ob_ref, dmat_ref, xi_ref, zeta_ref, state_ref):
    c = pl.program_id(1)
    ch = CHUNK

    @pl.when(c == 0)
    def _():
        state_ref[...] = jnp.zeros_like(state_ref)
        row = lax.broadcasted_iota(jnp.int32, (ch, ch), 0).astype(F32)
        col = lax.broadcasted_iota(jnp.int32, (ch, ch), 1).astype(F32)
        for h in range(N_HEADS):
            lg_f = _log_sigmoid(df_ref[h])[:, :ch]
            lg_b = _log_sigmoid(db_ref[h])[:, :ch]
            diff_f = row - col
            diff_b = col - row
            dmat_ref[0, h] = jnp.where(diff_f >= 0, jnp.exp(lg_f * jnp.maximum(diff_f, 0.0)), 0.0)
            dmat_ref[1, h] = jnp.where(diff_b > 0, jnp.exp(lg_b * jnp.maximum(diff_b, 0.0)), 0.0)
            xi_ref[0, h] = jnp.exp(lg_f * (row + 1.0))
            xi_ref[1, h] = jnp.exp(lg_b * (ch - row))
            zeta_ref[0, h] = jnp.exp(lg_f * (ch - 1.0 - row))
            zeta_ref[1, h] = jnp.exp(lg_b * row)

    def one(d, h, q_ref, k_ref, v_ref, o_ref, dec_ref):
        qs = slice(h * QK_DIM, (h + 1) * QK_DIM)
        vs = slice(h * V_DIM, (h + 1) * V_DIM)
        q = q_ref[:, qs]
        k = k_ref[:, qs]
        v = v_ref[:, vs]
        s = lax.dot_general(q, k, (((1,), (1,)), ((), ())), preferred_element_type=F32)
        s = (s * dmat_ref[d, h]).astype(BF16)
        qx = (q.astype(F32) * xi_ref[d, h]).astype(BF16)
        state = state_ref[d, h]
        o = (jnp.dot(s, v, preferred_element_type=F32)
             + jnp.dot(qx, state.astype(BF16), preferred_element_type=F32))
        o_ref[:, vs] = o
        kz = (k.astype(F32) * zeta_ref[d, h]).astype(BF16)
        upd = lax.dot_general(kz, v, (((0,), (0,)), ((), ())), preferred_element_type=F32)
        cd = jnp.exp(_log_sigmoid(dec_ref[h]) * float(ch))
        state_ref[d, h] = state * cd + upd

    for h in range(N_HEADS):
        one(0, h, qf_ref, kf_ref, vf_ref, of_ref, df_ref)
        one(1, h, qb_ref, kb_ref, vb_ref, ob_ref, db_ref)


def _retention(qk_arr, pv, dec_f, dec_b, nseq, seq):
    t = pv.shape[0]
    ch = CHUNK
    n = seq // ch
    assert seq % ch == 0 and t == nseq * seq

    def fwd(colblk):
        return lambda b, c: (b * n + c, colblk)

    def bwd(colblk):
        return lambda b, c: (b * n + (n - 1 - c), colblk)

    qk = lambda f, blk: pl.BlockSpec((ch, D_QK), f(blk))
    vv = lambda f: pl.BlockSpec((ch, D_V), f(BLK_V))
    dec = pl.BlockSpec((N_HEADS, 1, V_DIM), lambda b, c: (0, 0, 0))
    q_blk, k_blk = 0, 1
    out = jax.ShapeDtypeStruct((t, D_V), F32)
    return pl.pallas_call(
        _ret_kernel,
        grid=(nseq, n),
        in_specs=[qk(fwd, q_blk), qk(fwd, k_blk), vv(fwd),
                  qk(bwd, q_blk), qk(bwd, k_blk), vv(bwd), dec, dec],
        out_specs=[pl.BlockSpec((ch, D_V), fwd(0)), pl.BlockSpec((ch, D_V), bwd(0))],
        out_shape=[out, out],
        scratch_shapes=[pltpu.VMEM((2, N_HEADS, ch, ch), F32),
                        pltpu.VMEM((2, N_HEADS, ch, QK_DIM), F32),
                        pltpu.VMEM((2, N_HEADS, ch, QK_DIM), F32),
                        pltpu.VMEM((2, N_HEADS, QK_DIM, V_DIM), F32)],
        compiler_params=_params("arbitrary", "arbitrary"),
        name="retention",
    )(qk_arr, qk_arr, pv, qk_arr, qk_arr, pv, dec_f, dec_b)


def _merge_kernel(of_ref, ob_ref, g_ref, gc_ref, gr_ref, uc_ref, x_ref, gn_ref,
                  wr_ref, wc_ref, wo_ref, o_ref, rg_ref):
    for h in range(N_HEADS):
        vs = slice(h * V_DIM, (h + 1) * V_DIM)
        r = of_ref[:, vs] + ob_ref[:, vs]
        mu = jnp.mean(r, axis=-1, keepdims=True)
        d = r - mu
        var = jnp.mean(d * d, axis=-1, keepdims=True)
        rn = d * lax.rsqrt(var + EPS) * gn_ref[:, vs]
        rg_ref[:, vs] = (g_ref[:, vs].astype(F32) * rn).astype(BF16)
    y_ret = jnp.dot(rg_ref[...], wr_ref[...], preferred_element_type=F32)
    y_conv = jnp.dot(uc_ref[...], wc_ref[...], preferred_element_type=F32)
    mixed = gc_ref[...].astype(F32) * y_conv + gr_ref[...].astype(F32) * y_ret
    o_ref[...] = x_ref[...] + jnp.dot(mixed.astype(BF16), wo_ref[...],
                                      preferred_element_type=F32)


def _merge(o_f, o_b, p, uc, x1, gn_w, w_ret_o, w_conv_o, w_out):
    t = x1.shape[0]
    tm = MERGE_TM
    assert t % tm == 0
    row = lambda width, blk=0: pl.BlockSpec((tm, width), lambda i: (i, blk))
    full = lambda a: pl.BlockSpec(a.shape, lambda i: (0, 0))
    return pl.pallas_call(
        _merge_kernel,
        grid=(t // tm,),
        in_specs=[row(D_V), row(D_V), row(D_V, BLK_G), row(D_MODEL, BLK_GC),
                  row(D_MODEL, BLK_GR),
                  row(D_CONV), row(D_MODEL), full(gn_w),
                  full(w_ret_o), full(w_conv_o), full(w_out)],
        out_specs=row(D_MODEL),
        out_shape=jax.ShapeDtypeStruct((t, D_MODEL), F32),
        scratch_shapes=[pltpu.VMEM((tm, D_V), BF16)],
        compiler_params=_params("parallel"),
        name="merge",
    )(o_f, o_b, p, p, p, uc, x1, gn_w, w_ret_o, w_conv_o, w_out)


def _rope_tables(seq):
    half = QK_DIM // 2
    inv = ROPE_BASE ** (-jnp.arange(half, dtype=F32) / half)
    ang = jnp.arange(seq, dtype=F32)[:, None] * inv[None, :]
    cos, sin = jnp.cos(ang), jnp.sin(ang)
    return (jnp.concatenate([cos, cos], axis=-1),
            jnp.concatenate([-sin, sin], axis=-1))


def _trunk(x, wts):
    nseq, seq, _ = x.shape
    xf = x.reshape(nseq * seq, D_MODEL)
    x1, h = _ffn(xf, wts["ffn1_norm"], wts["ffn1_w1"], wts["ffn1_w3"], wts["ffn1_w2"],
                 wts["mix_norm"], emit_normed=True, name="ffn1")
    u = _glu_proj(h, wts["w_a"], wts["w_b"])
    cosf, sinf = _rope_tables(seq)
    qk = _qk_proj(h, wts["w_qk"], cosf, sinf, seq)
    pv = _vg_proj(h, wts["w_vg"])
    uc = _conv_branch(u, wts["dw_w"], wts["dw_b"], wts["conv_ln_w"], wts["conv_ln_b"], seq)
    o_f, o_b = _retention(qk, pv, wts["dec_f"], wts["dec_b"], nseq, seq)
    x2 = _merge(o_f, o_b, pv, uc, x1, wts["ret_gn_w"], wts["w_ret_o"], wts["w_conv_o"],
                wts["w_out"])
    (y,) = _ffn(x2, wts["ffn2_norm"], wts["ffn2_w1"], wts["ffn2_w3"], wts["ffn2_w2"],
                wts["final_norm"], emit_normed=False, name="ffn2")
    return y.reshape(nseq, seq, D_MODEL)


def kernel(x_prompt, x_sample, ffn1_norm, ffn1_w1, ffn1_w3, ffn1_w2, mix_norm, w_in, dw_w, dw_b,
           conv_ln_w, conv_ln_b, w_conv_o, decay_fwd, decay_bwd, ret_gn_w, w_ret_o, w_out,
           ffn2_norm, ffn2_w1, ffn2_w3, ffn2_w2, final_norm):
    assert ffn1_w1.shape[0] == 1, "single layer"
    bf = lambda a: a.astype(BF16)
    vec = lambda a: a.reshape(1, -1).astype(F32)
    w_in0 = w_in[0]
    dec = lambda a: jnp.broadcast_to(a[0].astype(F32)[:, None, None], (N_HEADS, 1, V_DIM))
    wts = dict(
        ffn1_norm=vec(ffn1_norm[0]), ffn1_w1=bf(ffn1_w1[0]), ffn1_w3=bf(ffn1_w3[0]),
        ffn1_w2=bf(ffn1_w2[0]), mix_norm=vec(mix_norm[0]),
        w_a=bf(w_in0[:, :D_CONV]), w_b=bf(w_in0[:, D_CONV:2 * D_CONV]),
        w_qk=bf(w_in0[:, 2 * D_CONV:2 * D_CONV + 2 * D_QK]),
        w_vg=bf(w_in0[:, 2 * D_CONV + 2 * D_QK:]),
        dw_w=dw_w[0].astype(F32), dw_b=vec(dw_b[0]),
        conv_ln_w=vec(conv_ln_w[0]), conv_ln_b=vec(conv_ln_b[0]),
        w_conv_o=bf(w_conv_o[0]), dec_f=dec(decay_fwd), dec_b=dec(decay_bwd),
        ret_gn_w=vec(ret_gn_w[0]), w_ret_o=bf(w_ret_o[0]), w_out=bf(w_out[0]),
        ffn2_norm=vec(ffn2_norm[0]), ffn2_w1=bf(ffn2_w1[0]), ffn2_w3=bf(ffn2_w3[0]),
        ffn2_w2=bf(ffn2_w2[0]), final_norm=vec(final_norm),
    )
    return (_trunk(x_prompt, wts), _trunk(x_sample, wts))
```

```python
import functools

import jax
import jax.numpy as jnp
from jax import lax
from jax.experimental import pallas as pl
from jax.experimental.pallas import tpu as pltpu

D_MODEL = 2048
D_CONV = 1024
CONV_WIDTH = 31
N_HEADS = 8
QK_DIM = 128
V_DIM = 256
D_QK = N_HEADS * QK_DIM
D_V = N_HEADS * V_DIM
CHUNK = 128
D_FF = 5632
ROPE_BASE = 10000.0
EPS = 1e-6

BF16 = jnp.bfloat16
F32 = jnp.float32

VMEM_LIMIT_BYTES = 56 * 1024 * 1024
SUBLANES = 8
LANES = 128

FFN_SPLITS = (1536, 1536, 1536, 1024)
FFN_TM = 512
FFN_LAST_TM = 256
PROJ_TM = 1024
PROJ_TN = 1024
GLU_TN = 512
CONV_TS = 128
CONV_HALO = 16
CONV_ROWS = 64
MERGE_TM = 256

VG_G, VG_GC = D_V // PROJ_TN, 2 * D_V // PROJ_TN
BLK_V, BLK_G, BLK_GC, BLK_GR = 0, 1, 2, 3

assert sum(FFN_SPLITS) == D_FF


def _params(*sem):
    return pltpu.CompilerParams(dimension_semantics=sem,
                                vmem_limit_bytes=VMEM_LIMIT_BYTES)


def _rms(x, g):
    ms = jnp.mean(x * x, axis=-1, keepdims=True)
    return x * lax.rsqrt(ms + EPS) * g


def _silu(x):
    return x * jax.nn.sigmoid(x)


def _full(a):
    return pl.BlockSpec(a.shape, lambda i: (0,) * a.ndim)


def _rows(tm, width, blk=0):
    return pl.BlockSpec((tm, width), lambda i: (i, blk))


def _swiglu_partial(hn, w1_ref, w3_ref, w2_ref):
    a = jnp.dot(hn, w1_ref[...], preferred_element_type=F32)
    b = jnp.dot(hn, w3_ref[...], preferred_element_type=F32)
    p = (_silu(a) * b).astype(BF16)
    return jnp.dot(p, w2_ref[...], preferred_element_type=F32)


def _ffn_first_kernel(x_ref, g_ref, w1_ref, w3_ref, w2_ref, acc_ref, hn_ref):
    hn = _rms(x_ref[...], g_ref[...]).astype(BF16)
    hn_ref[...] = hn
    acc_ref[...] = _swiglu_partial(hn, w1_ref, w3_ref, w2_ref)


def _ffn_mid_kernel(hn_ref, acc_in_ref, w1_ref, w3_ref, w2_ref, acc_ref):
    acc_ref[...] = acc_in_ref[...] + _swiglu_partial(hn_ref[...], w1_ref, w3_ref, w2_ref)


def _ffn_last_kernel(hn_ref, acc_in_ref, x_ref, g2_ref, w1_ref, w3_ref, w2_ref,
                     y_ref, *h_ref, emit_normed):
    s = acc_in_ref[...] + _swiglu_partial(hn_ref[...], w1_ref, w3_ref, w2_ref)
    y = x_ref[...] + 0.5 * s
    if emit_normed:
        y_ref[...] = y
        h_ref[0][...] = _rms(y, g2_ref[...]).astype(BF16)
    else:
        y_ref[...] = _rms(y, g2_ref[...])


def _ffn(x, g, w1s, w3s, w2s, g2, *, emit_normed, name):
    t = x.shape[0]
    tm = FFN_TM
    assert t % tm == 0 and t % FFN_LAST_TM == 0 and len(w1s) >= 2
    f32_out = jax.ShapeDtypeStruct((t, D_MODEL), F32)
    bf_out = jax.ShapeDtypeStruct((t, D_MODEL), BF16)
    row = _rows(tm, D_MODEL)

    acc, hn = pl.pallas_call(
        _ffn_first_kernel,
        grid=(t // tm,),
        in_specs=[row, _full(g), _full(w1s[0]), _full(w3s[0]), _full(w2s[0])],
        out_specs=[row, row],
        out_shape=[f32_out, bf_out],
        compiler_params=_params("parallel"),
        name=name + "_first",
    )(x, g, w1s[0], w3s[0], w2s[0])

    for w1, w3, w2 in zip(w1s[1:-1], w3s[1:-1], w2s[1:-1]):
        acc = pl.pallas_call(
            _ffn_mid_kernel,
            grid=(t // tm,),
            in_specs=[row, row, _full(w1), _full(w3), _full(w2)],
            out_specs=row,
            out_shape=f32_out,
            compiler_params=_params("parallel"),
            name=name + "_mid",
        )(hn, acc, w1, w3, w2)

    tl = FFN_LAST_TM
    rowl = _rows(tl, D_MODEL)
    return pl.pallas_call(
        functools.partial(_ffn_last_kernel, emit_normed=emit_normed),
        grid=(t // tl,),
        in_specs=[rowl, rowl, rowl, _full(g2), _full(w1s[-1]), _full(w3s[-1]), _full(w2s[-1])],
        out_specs=[rowl, rowl] if emit_normed else [rowl],
        out_shape=[f32_out, bf_out] if emit_normed else [f32_out],
        compiler_params=_params("parallel"),
        name=name + "_last",
    )(hn, acc, x, g2, w1s[-1], w3s[-1], w2s[-1])


def _glu_kernel(h_ref, wa_ref, wb_ref, u_ref):
    h = h_ref[...]
    a = jnp.dot(h, wa_ref[...], preferred_element_type=F32)
    b = jnp.dot(h, wb_ref[...], preferred_element_type=F32)
    u_ref[...] = a * jax.nn.sigmoid(b)


def _glu_proj(h, wa, wb):
    t = h.shape[0]
    tm, tn = PROJ_TM, GLU_TN
    assert t % tm == 0 and D_CONV % tn == 0
    wspec = pl.BlockSpec((D_MODEL, tn), lambda i, j: (0, j))
    return pl.pallas_call(
        _glu_kernel,
        grid=(t // tm, D_CONV // tn),
        in_specs=[pl.BlockSpec((tm, D_MODEL), lambda i, j: (i, 0)), wspec, wspec],
        out_specs=pl.BlockSpec((tm, tn), lambda i, j: (i, j)),
        out_shape=jax.ShapeDtypeStruct((t, D_CONV), F32),
        compiler_params=_params("parallel", "arbitrary"),
        name="glu_proj",
    )(h, wa, wb)


def _qk_proj_kernel(h_ref, w_ref, cos_ref, sin_ref, o_ref):
    scale = jnp.where(pl.program_id(1) == 1, QK_DIM ** -0.5, 1.0).astype(F32)
    y = jnp.dot(h_ref[...], w_ref[...], preferred_element_type=F32)
    cos = cos_ref[...]
    sin = sin_ref[...]
    for hd in range(D_QK // QK_DIM):
        sl = slice(hd * QK_DIM, (hd + 1) * QK_DIM)
        xh = y[:, sl]
        rot = xh * cos + pltpu.roll(xh, QK_DIM // 2, 1) * sin
        o_ref[:, sl] = (rot * scale).astype(BF16)


def _dwconv_ln_silu(prev_ref, cur_ref, next_ref, w_ref, b_ref, lnw_ref, lnb_ref,
                    pad_ref, y_ref, il, tiles_per_seq):
    ts = cur_ref.shape[0]
    pad_ref[0:CONV_HALO, :] = jnp.where(il > 0, prev_ref[...], 0.0)
    pad_ref[CONV_HALO:CONV_HALO + ts, :] = cur_ref[...]
    pad_ref[CONV_HALO + ts:, :] = jnp.where(il < tiles_per_seq - 1, next_ref[...], 0.0)

    rows = CONV_ROWS
    for r0 in range(0, ts, rows):
        for c0 in range(0, D_CONV, LANES):
            cs = slice(c0, c0 + LANES)
            acc = None
            for b in range(SUBLANES):
                part = None
                for a in range(CONV_WIDTH // SUBLANES + 1):
                    s = SUBLANES * a + b
                    if s < 1 or s > CONV_WIDTH:
                        continue
                    xs = pad_ref[r0 + SUBLANES * a:r0 + SUBLANES * a + rows + SUBLANES, cs]
                    term = xs * w_ref[s - 1:s, cs]
                    part = term if part is None else part + term
                part = part[b:b + rows, :]
                acc = part if acc is None else acc + part
            y_ref[r0:r0 + rows, cs] = acc

    y = y_ref[...] + b_ref[...]
    mu = jnp.mean(y, axis=-1, keepdims=True)
    d = y - mu
    var = jnp.mean(d * d, axis=-1, keepdims=True)
    z = d * lax.rsqrt(var + EPS) * lnw_ref[...] + lnb_ref[...]
    return _silu(z).astype(BF16)


def _conv_kernel(prev_ref, cur_ref, next_ref, w_ref, b_ref, lnw_ref, lnb_ref,
                 o_ref, pad_ref, y_ref, *, tiles_per_seq):
    il = pl.program_id(0) % tiles_per_seq
    o_ref[...] = _dwconv_ln_silu(prev_ref, cur_ref, next_ref, w_ref, b_ref, lnw_ref, lnb_ref,
                                 pad_ref, y_ref, il, tiles_per_seq)


def _conv_branch(u, wts, seq):
    t = u.shape[0]
    ts = CONV_TS
    assert seq % ts == 0 and ts % CONV_HALO == 0 and CONV_HALO > CONV_WIDTH // 2
    assert ts % CONV_ROWS == 0
    hb = ts // CONV_HALO
    last = t // CONV_HALO - 1
    small = [wts["dw_w"], wts["dw_b"], wts["conv_ln_w"], wts["conv_ln_b"]]
    return pl.pallas_call(
        functools.partial(_conv_kernel, tiles_per_seq=seq // ts),
        grid=(t // ts,),
        in_specs=[
            pl.BlockSpec((CONV_HALO, D_CONV), lambda i: (jnp.maximum(i * hb - 1, 0), 0)),
            _rows(ts, D_CONV),
            pl.BlockSpec((CONV_HALO, D_CONV), lambda i: (jnp.minimum((i + 1) * hb, last), 0))]
            + [_full(a) for a in small],
        out_specs=_rows(ts, D_CONV),
        out_shape=jax.ShapeDtypeStruct((t, D_CONV), BF16),
        scratch_shapes=[pltpu.VMEM((ts + 2 * CONV_HALO, D_CONV), F32),
                        pltpu.VMEM((ts, D_CONV), F32)],
        compiler_params=_params("parallel"),
        name="conv_branch",
    )(u, u, u, *small)


def _vg_proj_kernel(h_ref, w_ref, o_ref):
    j = pl.program_id(1)
    y = jnp.dot(h_ref[...], w_ref[...], preferred_element_type=F32)
    sig = jax.nn.sigmoid(y)
    out = jnp.where(j < VG_G, y, jnp.where(j < VG_GC, y * sig, sig))
    o_ref[...] = out.astype(BF16)


def _proj(kernel_fn, h, w, extra, extra_specs, name):
    t = h.shape[0]
    n = w.shape[1]
    tm, tn = PROJ_TM, PROJ_TN
    assert t % tm == 0 and n % tn == 0
    return pl.pallas_call(
        kernel_fn,
        grid=(t // tm, n // tn),
        in_specs=[pl.BlockSpec((tm, D_MODEL), lambda i, j: (i, 0)),
                  pl.BlockSpec((D_MODEL, tn), lambda i, j: (0, j))] + extra_specs,
        out_specs=pl.BlockSpec((tm, tn), lambda i, j: (i, j)),
        out_shape=jax.ShapeDtypeStruct((t, n), BF16),
        compiler_params=_params("parallel", "arbitrary"),
        name=name,
    )(h, w, *extra)


def _qk_proj(h, w, cosf, sinf, seq):
    assert seq % PROJ_TM == 0 and PROJ_TN == D_QK
    tiles_per_seq = seq // PROJ_TM
    tab = pl.BlockSpec((PROJ_TM, QK_DIM), lambda i, j: (i % tiles_per_seq, 0))
    return _proj(_qk_proj_kernel, h, w, [cosf, sinf], [tab, tab], "qk_proj")


def _vg_proj(h, w):
    return _proj(_vg_proj_kernel, h, w, [], [], "vg_proj")


def _log_sigmoid(x):
    return -(jnp.maximum(-x, 0.0) + jnp.log1p(jnp.exp(-jnp.abs(x))))


def _ret_kernel(qf_ref, kf_ref, vf_ref, qb_ref, kb_ref, vb_ref, df_ref, db_ref,
                of_ref, ob_ref, dmat_ref, xi_ref, zeta_ref, state_ref):
    c = pl.program_id(1)
    ch = CHUNK

    @pl.when(c == 0)
    def _():
        state_ref[...] = jnp.zeros_like(state_ref)
        row = lax.broadcasted_iota(jnp.int32, (ch, ch), 0).astype(F32)
        col = lax.broadcasted_iota(jnp.int32, (ch, ch), 1).astype(F32)
        rowq = lax.broadcasted_iota(jnp.int32, (ch, QK_DIM), 0).astype(F32)
        for h in range(N_HEADS):
            lg_f = _log_sigmoid(df_ref[h])[:, :ch]
            lg_b = _log_sigmoid(db_ref[h])[:, :ch]
            lq_f = _log_sigmoid(df_ref[h])[:, :QK_DIM]
            lq_b = _log_sigmoid(db_ref[h])[:, :QK_DIM]
            diff_f = row - col
            diff_b = col - row
            dmat_ref[0, h] = jnp.where(diff_f >= 0, jnp.exp(lg_f * jnp.maximum(diff_f, 0.0)), 0.0)
            dmat_ref[1, h] = jnp.where(diff_b > 0, jnp.exp(lg_b * jnp.maximum(diff_b, 0.0)), 0.0)
            xi_ref[0, h] = jnp.exp(lq_f * (rowq + 1.0))
            xi_ref[1, h] = jnp.exp(lq_b * (ch - rowq))
            zeta_ref[0, h] = jnp.exp(lq_f * (ch - 1.0 - rowq))
            zeta_ref[1, h] = jnp.exp(lq_b * rowq)

    def one(d, h, q_ref, k_ref, v_ref, o_ref, dec_ref):
        qs = slice(h * QK_DIM, (h + 1) * QK_DIM)
        vs = slice(h * V_DIM, (h + 1) * V_DIM)
        q = q_ref[:, qs]
        k = k_ref[:, qs]
        v = v_ref[:, vs]
        s = lax.dot_general(q, k, (((1,), (1,)), ((), ())), preferred_element_type=F32)
        s = (s * dmat_ref[d, h]).astype(BF16)
        qx = (q.astype(F32) * xi_ref[d, h]).astype(BF16)
        state = state_ref[d, h]
        o = (jnp.dot(s, v, preferred_element_type=F32)
             + jnp.dot(qx, state.astype(BF16), preferred_element_type=F32))
        o_ref[:, vs] = o.astype(BF16)
        kz = (k.astype(F32) * zeta_ref[d, h]).astype(BF16)
        upd = lax.dot_general(kz, v, (((0,), (0,)), ((), ())), preferred_element_type=F32)
        cd = jnp.exp(_log_sigmoid(dec_ref[h]) * float(ch))
        state_ref[d, h] = state * cd + upd

    for h in range(N_HEADS):
        one(0, h, qf_ref, kf_ref, vf_ref, of_ref, df_ref)
        one(1, h, qb_ref, kb_ref, vb_ref, ob_ref, db_ref)


def _retention(qk_arr, pv, dec_f, dec_b, nseq, seq):
    t = pv.shape[0]
    ch = CHUNK
    n = seq // ch
    assert seq % ch == 0 and t == nseq * seq

    def fwd(colblk):
        return lambda b, c: (b * n + c, colblk)

    def bwd(colblk):
        return lambda b, c: (b * n + (n - 1 - c), colblk)

    qk = lambda f, blk: pl.BlockSpec((ch, D_QK), f(blk))
    vv = lambda f: pl.BlockSpec((ch, D_V), f(BLK_V))
    dec = pl.BlockSpec((N_HEADS, 1, V_DIM), lambda b, c: (0, 0, 0))
    q_blk, k_blk = 0, 1
    out = jax.ShapeDtypeStruct((t, D_V), BF16)
    return pl.pallas_call(
        _ret_kernel,
        grid=(nseq, n),
        in_specs=[qk(fwd, q_blk), qk(fwd, k_blk), vv(fwd),
                  qk(bwd, q_blk), qk(bwd, k_blk), vv(bwd), dec, dec],
        out_specs=[pl.BlockSpec((ch, D_V), fwd(0)), pl.BlockSpec((ch, D_V), bwd(0))],
        out_shape=[out, out],
        scratch_shapes=[pltpu.VMEM((2, N_HEADS, ch, ch), F32),
                        pltpu.VMEM((2, N_HEADS, ch, QK_DIM), F32),
                        pltpu.VMEM((2, N_HEADS, ch, QK_DIM), F32),
                        pltpu.VMEM((2, N_HEADS, QK_DIM, V_DIM), F32)],
        compiler_params=_params("arbitrary", "arbitrary"),
        name="retention",
    )(qk_arr, qk_arr, pv, qk_arr, qk_arr, pv, dec_f, dec_b)


def _merge_kernel(of_ref, ob_ref, g_ref, gc_ref, gr_ref, uc_ref, x_ref, gn_ref,
                  wr_ref, wc_ref, wo_ref, o_ref, rg_ref):
    for h in range(N_HEADS):
        vs = slice(h * V_DIM, (h + 1) * V_DIM)
        r = of_ref[:, vs].astype(F32) + ob_ref[:, vs].astype(F32)
        mu = jnp.mean(r, axis=-1, keepdims=True)
        d = r - mu
        var = jnp.mean(d * d, axis=-1, keepdims=True)
        rn = d * lax.rsqrt(var + EPS) * gn_ref[:, vs]
        rg_ref[:, vs] = (g_ref[:, vs].astype(F32) * rn).astype(BF16)
    y_ret = jnp.dot(rg_ref[...], wr_ref[...], preferred_element_type=F32)
    y_conv = jnp.dot(uc_ref[...], wc_ref[...], preferred_element_type=F32)
    mixed = gc_ref[...].astype(F32) * y_conv + gr_ref[...].astype(F32) * y_ret
    o_ref[...] = x_ref[...] + jnp.dot(mixed.astype(BF16), wo_ref[...],
                                      preferred_element_type=F32)


def _merge(o_f, o_b, pv, uc, x1, wts):
    t = x1.shape[0]
    tm = MERGE_TM
    assert t % tm == 0
    big = [wts["ret_gn_w"], wts["w_ret_o"], wts["w_conv_o"], wts["w_out"]]
    return pl.pallas_call(
        _merge_kernel,
        grid=(t // tm,),
        in_specs=[_rows(tm, D_V), _rows(tm, D_V), _rows(tm, D_V, BLK_G),
                  _rows(tm, D_MODEL, BLK_GC), _rows(tm, D_MODEL, BLK_GR),
                  _rows(tm, D_CONV), _rows(tm, D_MODEL)] + [_full(a) for a in big],
        out_specs=_rows(tm, D_MODEL),
        out_shape=jax.ShapeDtypeStruct((t, D_MODEL), F32),
        scratch_shapes=[pltpu.VMEM((tm, D_V), BF16)],
        compiler_params=_params("parallel"),
        name="merge",
    )(o_f, o_b, pv, pv, pv, uc, x1, *big)


def _rope_tables(seq):
    half = QK_DIM // 2
    inv = ROPE_BASE ** (-jnp.arange(half, dtype=F32) / half)
    ang = jnp.arange(seq, dtype=F32)[:, None] * inv[None, :]
    cos, sin = jnp.cos(ang), jnp.sin(ang)
    return (jnp.concatenate([cos, cos], axis=-1),
            jnp.concatenate([-sin, sin], axis=-1))


def _trunk(x, wts):
    nseq, seq, _ = x.shape
    xf = x.reshape(nseq * seq, D_MODEL)
    x1, h = _ffn(xf, wts["ffn1_norm"], wts["ffn1_w1"], wts["ffn1_w3"], wts["ffn1_w2"],
                 wts["mix_norm"], emit_normed=True, name="ffn1")
    u = _glu_proj(h, wts["w_a"], wts["w_b"])
    cosf, sinf = _rope_tables(seq)
    qk = _qk_proj(h, wts["w_qk"], cosf, sinf, seq)
    pv = _vg_proj(h, wts["w_vg"])
    uc = _conv_branch(u, wts, seq)
    o_f, o_b = _retention(qk, pv, wts["dec_f"], wts["dec_b"], nseq, seq)
    x2 = _merge(o_f, o_b, pv, uc, x1, wts)
    (y,) = _ffn(x2, wts["ffn2_norm"], wts["ffn2_w1"], wts["ffn2_w3"], wts["ffn2_w2"],
                wts["final_norm"], emit_normed=False, name="ffn2")
    return y.reshape(nseq, seq, D_MODEL)


def _split_cols(w):
    out, lo = [], 0
    for n in FFN_SPLITS:
        out.append(w[:, lo:lo + n].astype(BF16))
        lo += n
    return out


def _split_rows(w):
    out, lo = [], 0
    for n in FFN_SPLITS:
        out.append(w[lo:lo + n, :].astype(BF16))
        lo += n
    return out


def kernel(x_prompt, x_sample, ffn1_norm, ffn1_w1, ffn1_w3, ffn1_w2, mix_norm, w_in, dw_w, dw_b,
           conv_ln_w, conv_ln_b, w_conv_o, decay_fwd, decay_bwd, ret_gn_w, w_ret_o, w_out,
           ffn2_norm, ffn2_w1, ffn2_w3, ffn2_w2, final_norm):
    assert ffn1_w1.shape[0] == 1, "single layer"
    bf = lambda a: a.astype(BF16)
    vec = lambda a: a.reshape(1, -1).astype(F32)
    w_in0 = w_in[0]
    dec = lambda a: jnp.broadcast_to(a[0].astype(F32)[:, None, None], (N_HEADS, 1, V_DIM))
    wts = dict(
        ffn1_norm=vec(ffn1_norm[0]), ffn1_w1=_split_cols(ffn1_w1[0]),
        ffn1_w3=_split_cols(ffn1_w3[0]), ffn1_w2=_split_rows(ffn1_w2[0]),
        mix_norm=vec(mix_norm[0]),
        w_a=bf(w_in0[:, :D_CONV]), w_b=bf(w_in0[:, D_CONV:2 * D_CONV]),
        w_qk=bf(w_in0[:, 2 * D_CONV:2 * D_CONV + 2 * D_QK]),
        w_vg=bf(w_in0[:, 2 * D_CONV + 2 * D_QK:]),
        dw_w=dw_w[0].astype(F32), dw_b=vec(dw_b[0]),
        conv_ln_w=vec(conv_ln_w[0]), conv_ln_b=vec(conv_ln_b[0]),
        w_conv_o=bf(w_conv_o[0]), dec_f=dec(decay_fwd), dec_b=dec(decay_bwd),
        ret_gn_w=vec(ret_gn_w[0]), w_ret_o=bf(w_ret_o[0]), w_out=bf(w_out[0]),
        ffn2_norm=vec(ffn2_norm[0]), ffn2_w1=_split_cols(ffn2_w1[0]),
        ffn2_w3=_split_cols(ffn2_w3[0]), ffn2_w2=_split_rows(ffn2_w2[0]),
        final_norm=vec(final_norm),
    )
    return (_trunk(x_prompt, wts), _trunk(x_sample, wts))
```

```python
import functools

import jax
import jax.numpy as jnp
from jax import lax
from jax.experimental import pallas as pl
from jax.experimental.pallas import tpu as pltpu

D_MODEL = 2048
D_CONV = 1024
CONV_WIDTH = 31
N_HEADS = 8
QK_DIM = 128
V_DIM = 256
D_QK = N_HEADS * QK_DIM
D_V = N_HEADS * V_DIM
CHUNK = 128
D_FF = 5632
ROPE_BASE = 10000.0
EPS = 1e-6

BF16 = jnp.bfloat16
F32 = jnp.float32

VMEM_LIMIT_BYTES = 56 * 1024 * 1024
SUBLANES = 8
LANES = 128

FFN_SPLITS = (2304, 2304, 1024)
FFN_TM = 512
FFN_LAST_TM = 512
PROJ_TM = 1024
CONV_TS = 128
CONV_HALO = 16
CONV_ROWS = 64
MIX_TM = 512
OUT_TM = 1024

assert sum(FFN_SPLITS) == D_FF


def _params(*sem):
    return pltpu.CompilerParams(dimension_semantics=sem,
                                vmem_limit_bytes=VMEM_LIMIT_BYTES)


def _rms(x, g):
    ms = jnp.mean(x * x, axis=-1, keepdims=True)
    return x * lax.rsqrt(ms + EPS) * g


def _silu(x):
    return x * jax.nn.sigmoid(x)


def _full(a):
    return pl.BlockSpec(a.shape, lambda i: (0,) * a.ndim)


def _rows(tm, width, blk=0):
    return pl.BlockSpec((tm, width), lambda i: (i, blk))


def _swiglu_partial(hn, w1_ref, w3_ref, w2_ref):
    a = jnp.dot(hn, w1_ref[...], preferred_element_type=F32)
    b = jnp.dot(hn, w3_ref[...], preferred_element_type=F32)
    p = (_silu(a) * b).astype(BF16)
    return jnp.dot(p, w2_ref[...], preferred_element_type=F32)


def _ffn_first_kernel(x_ref, g_ref, w1_ref, w3_ref, w2_ref, acc_ref, hn_ref):
    hn = _rms(x_ref[...], g_ref[...]).astype(BF16)
    hn_ref[...] = hn
    acc_ref[...] = _swiglu_partial(hn, w1_ref, w3_ref, w2_ref)


def _ffn_mid_kernel(hn_ref, acc_in_ref, w1_ref, w3_ref, w2_ref, acc_ref):
    acc_ref[...] = acc_in_ref[...] + _swiglu_partial(hn_ref[...], w1_ref, w3_ref, w2_ref)


def _ffn_last_kernel(hn_ref, acc_in_ref, x_ref, g2_ref, w1_ref, w3_ref, w2_ref,
                     y_ref, *h_ref, emit_normed):
    s = acc_in_ref[...] + _swiglu_partial(hn_ref[...], w1_ref, w3_ref, w2_ref)
    y = x_ref[...] + 0.5 * s
    if emit_normed:
        y_ref[...] = y
        h_ref[0][...] = _rms(y, g2_ref[...]).astype(BF16)
    else:
        y_ref[...] = _rms(y, g2_ref[...])


def _ffn(x, g, w1s, w3s, w2s, g2, *, emit_normed, name):
    t = x.shape[0]
    tm = FFN_TM
    assert t % tm == 0 and t % FFN_LAST_TM == 0 and len(w1s) >= 2
    f32_out = jax.ShapeDtypeStruct((t, D_MODEL), F32)
    bf_out = jax.ShapeDtypeStruct((t, D_MODEL), BF16)
    row = _rows(tm, D_MODEL)

    acc, hn = pl.pallas_call(
        _ffn_first_kernel,
        grid=(t // tm,),
        in_specs=[row, _full(g), _full(w1s[0]), _full(w3s[0]), _full(w2s[0])],
        out_specs=[row, row],
        out_shape=[f32_out, bf_out],
        compiler_params=_params("parallel"),
        name=name + "_first",
    )(x, g, w1s[0], w3s[0], w2s[0])

    for w1, w3, w2 in zip(w1s[1:-1], w3s[1:-1], w2s[1:-1]):
        acc = pl.pallas_call(
            _ffn_mid_kernel,
            grid=(t // tm,),
            in_specs=[row, row, _full(w1), _full(w3), _full(w2)],
            out_specs=row,
            out_shape=f32_out,
            compiler_params=_params("parallel"),
            name=name + "_mid",
        )(hn, acc, w1, w3, w2)

    tl = FFN_LAST_TM
    rowl = _rows(tl, D_MODEL)
    return pl.pallas_call(
        functools.partial(_ffn_last_kernel, emit_normed=emit_normed),
        grid=(t // tl,),
        in_specs=[rowl, rowl, rowl, _full(g2), _full(w1s[-1]), _full(w3s[-1]), _full(w2s[-1])],
        out_specs=[rowl, rowl] if emit_normed else [rowl],
        out_shape=[f32_out, bf_out] if emit_normed else [f32_out],
        compiler_params=_params("parallel"),
        name=name + "_last",
    )(hn, acc, x, g2, w1s[-1], w3s[-1], w2s[-1])


def _hdot(h, w_ref, lo, n):
    return jnp.dot(h, w_ref[:, lo:lo + n], preferred_element_type=F32)


def _convqk_proj_kernel(h_ref, wab_ref, wqk_ref, cos_ref, sin_ref, u_ref, qk_ref):
    h = h_ref[...]
    u_ref[...] = _hdot(h, wab_ref, 0, D_CONV) * jax.nn.sigmoid(_hdot(h, wab_ref, D_CONV, D_CONV))
    cos = cos_ref[...]
    sin = sin_ref[...]
    for part, scale in ((0, 1.0), (1, QK_DIM ** -0.5)):
        y = _hdot(h, wqk_ref, part * D_QK, D_QK)
        for hd in range(N_HEADS):
            sl = slice(hd * QK_DIM, (hd + 1) * QK_DIM)
            xh = y[:, sl]
            rot = xh * cos + pltpu.roll(xh, QK_DIM // 2, 1) * sin
            qk_ref[:, part * D_QK + hd * QK_DIM:part * D_QK + (hd + 1) * QK_DIM] = (
                rot * scale).astype(BF16)


def _vg_proj_kernel(h_ref, w_ref, o_ref):
    h = h_ref[...]
    o_ref[:, :D_V] = _hdot(h, w_ref, 0, D_V).astype(BF16)
    o_ref[:, D_V:] = _silu(_hdot(h, w_ref, D_V, D_V)).astype(BF16)


def _gate_proj_kernel(h_ref, w_ref, o_ref):
    h = h_ref[...]
    for lo in range(0, 2 * D_MODEL, D_MODEL):
        o_ref[:, lo:lo + D_MODEL] = jax.nn.sigmoid(_hdot(h, w_ref, lo, D_MODEL)).astype(BF16)


def _in_proj(h, wts, cosf, sinf, seq):
    t = h.shape[0]
    tm = PROJ_TM
    assert t % tm == 0 and seq % tm == 0
    tiles_per_seq = seq // tm
    hrow = _rows(tm, D_MODEL)
    tab = pl.BlockSpec((tm, QK_DIM), lambda i: (i % tiles_per_seq, 0))
    u, qk = pl.pallas_call(
        _convqk_proj_kernel,
        grid=(t // tm,),
        in_specs=[hrow, _full(wts["w_ab"]), _full(wts["w_qk"]), tab, tab],
        out_specs=[_rows(tm, D_CONV), _rows(tm, 2 * D_QK)],
        out_shape=[jax.ShapeDtypeStruct((t, D_CONV), F32),
                   jax.ShapeDtypeStruct((t, 2 * D_QK), BF16)],
        compiler_params=_params("parallel"),
        name="convqk_proj",
    )(h, wts["w_ab"], wts["w_qk"], cosf, sinf)
    wide = []
    for fn, w, name in ((_vg_proj_kernel, wts["w_vg"], "vg_proj"),
                        (_gate_proj_kernel, wts["w_gates"], "gate_proj")):
        wide.append(pl.pallas_call(
            fn,
            grid=(t // tm,),
            in_specs=[hrow, _full(w)],
            out_specs=_rows(tm, w.shape[1]),
            out_shape=jax.ShapeDtypeStruct((t, w.shape[1]), BF16),
            compiler_params=_params("parallel"),
            name=name,
        )(h, w))
    return u, qk, wide[0], wide[1]


def _dwconv_ln_silu(prev_ref, cur_ref, next_ref, w_ref, b_ref, lnw_ref, lnb_ref,
                    pad_ref, y_ref, il, tiles_per_seq):
    ts = cur_ref.shape[0]
    pad_ref[0:CONV_HALO, :] = jnp.where(il > 0, prev_ref[...], 0.0)
    pad_ref[CONV_HALO:CONV_HALO + ts, :] = cur_ref[...]
    pad_ref[CONV_HALO + ts:, :] = jnp.where(il < tiles_per_seq - 1, next_ref[...], 0.0)

    rows = CONV_ROWS
    for r0 in range(0, ts, rows):
        for c0 in range(0, D_CONV, LANES):
            cs = slice(c0, c0 + LANES)
            acc = None
            for b in range(SUBLANES):
                part = None
                for a in range(CONV_WIDTH // SUBLANES + 1):
                    s = SUBLANES * a + b
                    if s < 1 or s > CONV_WIDTH:
                        continue
                    xs = pad_ref[r0 + SUBLANES * a:r0 + SUBLANES * a + rows + SUBLANES, cs]
                    term = xs * w_ref[s - 1:s, cs]
                    part = term if part is None else part + term
                part = part[b:b + rows, :]
                acc = part if acc is None else acc + part
            y_ref[r0:r0 + rows, cs] = acc

    y = y_ref[...] + b_ref[...]
    mu = jnp.mean(y, axis=-1, keepdims=True)
    d = y - mu
    var = jnp.mean(d * d, axis=-1, keepdims=True)
    z = d * lax.rsqrt(var + EPS) * lnw_ref[...] + lnb_ref[...]
    return _silu(z).astype(BF16)


def _conv_kernel(prev_ref, cur_ref, next_ref, w_ref, b_ref, lnw_ref, lnb_ref,
                 o_ref, pad_ref, y_ref, *, tiles_per_seq):
    il = pl.program_id(0) % tiles_per_seq
    o_ref[...] = _dwconv_ln_silu(prev_ref, cur_ref, next_ref, w_ref, b_ref, lnw_ref, lnb_ref,
                                 pad_ref, y_ref, il, tiles_per_seq)


def _conv_branch(u, wts, seq):
    t = u.shape[0]
    ts = CONV_TS
    assert seq % ts == 0 and ts % CONV_HALO == 0 and CONV_HALO > CONV_WIDTH // 2
    assert ts % CONV_ROWS == 0
    hb = ts // CONV_HALO
    last = t // CONV_HALO - 1
    small = [wts["dw_w"], wts["dw_b"], wts["conv_ln_w"], wts["conv_ln_b"]]
    return pl.pallas_call(
        functools.partial(_conv_kernel, tiles_per_seq=seq // ts),
        grid=(t // ts,),
        in_specs=[
            pl.BlockSpec((CONV_HALO, D_CONV), lambda i: (jnp.maximum(i * hb - 1, 0), 0)),
            _rows(ts, D_CONV),
            pl.BlockSpec((CONV_HALO, D_CONV), lambda i: (jnp.minimum((i + 1) * hb, last), 0))]
            + [_full(a) for a in small],
        out_specs=_rows(ts, D_CONV),
        out_shape=jax.ShapeDtypeStruct((t, D_CONV), BF16),
        scratch_shapes=[pltpu.VMEM((ts + 2 * CONV_HALO, D_CONV), F32),
                        pltpu.VMEM((ts, D_CONV), F32)],
        compiler_params=_params("parallel"),
        name="conv_branch",
    )(u, u, u, *small)


def _log_sigmoid(x):
    return -(jnp.maximum(-x, 0.0) + jnp.log1p(jnp.exp(-jnp.abs(x))))


def _ret_kernel(qf_ref, kf_ref, vf_ref, qb_ref, kb_ref, vb_ref, df_ref, db_ref,
                of_ref, ob_ref, dmat_ref, xi_ref, zeta_ref, state_ref):
    c = pl.program_id(1)
    ch = CHUNK

    @pl.when(c == 0)
    def _():
        state_ref[...] = jnp.zeros_like(state_ref)
        row = lax.broadcasted_iota(jnp.int32, (ch, ch), 0).astype(F32)
        col = lax.broadcasted_iota(jnp.int32, (ch, ch), 1).astype(F32)
        rowq = lax.broadcasted_iota(jnp.int32, (ch, QK_DIM), 0).astype(F32)
        for h in range(N_HEADS):
            lg_f = _log_sigmoid(df_ref[h])[:, :ch]
            lg_b = _log_sigmoid(db_ref[h])[:, :ch]
            lq_f = _log_sigmoid(df_ref[h])[:, :QK_DIM]
            lq_b = _log_sigmoid(db_ref[h])[:, :QK_DIM]
            diff_f = row - col
            diff_b = col - row
            dmat_ref[0, h] = jnp.where(diff_f >= 0, jnp.exp(lg_f * jnp.maximum(diff_f, 0.0)), 0.0)
            dmat_ref[1, h] = jnp.where(diff_b > 0, jnp.exp(lg_b * jnp.maximum(diff_b, 0.0)), 0.0)
            xi_ref[0, h] = jnp.exp(lq_f * (rowq + 1.0))
            xi_ref[1, h] = jnp.exp(lq_b * (ch - rowq))
            zeta_ref[0, h] = jnp.exp(lq_f * (ch - 1.0 - rowq))
            zeta_ref[1, h] = jnp.exp(lq_b * rowq)

    dirs = ((0, qf_ref, kf_ref, vf_ref, of_ref, df_ref),
            (1, qb_ref, kb_ref, vb_ref, ob_ref, db_ref))
    work = [(d, h) + tuple(refs) for h in range(N_HEADS) for (d, *refs) in dirs]
    qs = lambda h: slice(h * QK_DIM, (h + 1) * QK_DIM)
    vs = lambda h: slice(h * V_DIM, (h + 1) * V_DIM)

    scores = [lax.dot_general(q[:, qs(h)], k[:, qs(h)], (((1,), (1,)), ((), ())),
                              preferred_element_type=F32)
              for (d, h, q, k, v, o, dec) in work]
    scores = [(sc * dmat_ref[d, h]).astype(BF16) for sc, (d, h, *_) in zip(scores, work)]

    for sc, (d, h, q, k, v, o, dec) in zip(scores, work):
        qx = (q[:, qs(h)].astype(F32) * xi_ref[d, h]).astype(BF16)
        lhs = jnp.concatenate([sc, qx], axis=1)
        rhs = jnp.concatenate([v[:, vs(h)], state_ref[d, h].astype(BF16)], axis=0)
        o[:, vs(h)] = jnp.dot(lhs, rhs, preferred_element_type=F32).astype(BF16)

    for (d, h, q, k, v, o, dec) in work:
        kz = (k[:, qs(h)].astype(F32) * zeta_ref[d, h]).astype(BF16)
        upd = lax.dot_general(kz, v[:, vs(h)], (((0,), (0,)), ((), ())),
                              preferred_element_type=F32)
        cd = jnp.exp(_log_sigmoid(dec[h]) * float(ch))
        state_ref[d, h] = state_ref[d, h] * cd + upd


def _retention(qk_arr, vg, dec_f, dec_b, nseq, seq):
    t = vg.shape[0]
    ch = CHUNK
    n = seq // ch
    assert seq % ch == 0 and t == nseq * seq

    def fwd(colblk):
        return lambda b, c: (b * n + c, colblk)

    def bwd(colblk):
        return lambda b, c: (b * n + (n - 1 - c), colblk)

    qk = lambda f, blk: pl.BlockSpec((ch, D_QK), f(blk))
    vv = lambda f: pl.BlockSpec((ch, D_V), f(0))
    dec = pl.BlockSpec((N_HEADS, 1, V_DIM), lambda b, c: (0, 0, 0))
    q_blk, k_blk = 0, 1
    out = jax.ShapeDtypeStruct((t, D_V), BF16)
    return pl.pallas_call(
        _ret_kernel,
        grid=(nseq, n),
        in_specs=[qk(fwd, q_blk), qk(fwd, k_blk), vv(fwd),
                  qk(bwd, q_blk), qk(bwd, k_blk), vv(bwd), dec, dec],
        out_specs=[pl.BlockSpec((ch, D_V), fwd(0)), pl.BlockSpec((ch, D_V), bwd(0))],
        out_shape=[out, out],
        scratch_shapes=[pltpu.VMEM((2, N_HEADS, ch, ch), F32),
                        pltpu.VMEM((2, N_HEADS, ch, QK_DIM), F32),
                        pltpu.VMEM((2, N_HEADS, ch, QK_DIM), F32),
                        pltpu.VMEM((2, N_HEADS, QK_DIM, V_DIM), F32)],
        compiler_params=_params("arbitrary", "arbitrary"),
        name="retention",
    )(qk_arr, qk_arr, vg, qk_arr, qk_arr, vg, dec_f, dec_b)


def _mix_kernel(of_ref, ob_ref, g_ref, gc_ref, gr_ref, uc_ref, gn_ref, wr_ref, wc_ref,
                o_ref, rg_ref):
    for h in range(N_HEADS):
        vs = slice(h * V_DIM, (h + 1) * V_DIM)
        r = of_ref[:, vs].astype(F32) + ob_ref[:, vs].astype(F32)
        mu = jnp.mean(r, axis=-1, keepdims=True)
        d = r - mu
        var = jnp.mean(d * d, axis=-1, keepdims=True)
        rn = d * lax.rsqrt(var + EPS) * gn_ref[:, vs]
        rg_ref[:, vs] = (g_ref[:, vs].astype(F32) * rn).astype(BF16)
    y_ret = jnp.dot(rg_ref[...], wr_ref[...], preferred_element_type=F32)
    y_conv = jnp.dot(uc_ref[...], wc_ref[...], preferred_element_type=F32)
    mixed = gc_ref[...].astype(F32) * y_conv + gr_ref[...].astype(F32) * y_ret
    o_ref[...] = mixed.astype(BF16)


def _out_kernel(m_ref, x_ref, wo_ref, o_ref):
    o_ref[...] = x_ref[...] + jnp.dot(m_ref[...], wo_ref[...], preferred_element_type=F32)


def _merge(o_f, o_b, vg, gates, uc, x1, wts):
    t = x1.shape[0]
    tm = MIX_TM
    assert t % tm == 0 and t % OUT_TM == 0
    big = [wts["ret_gn_w"], wts["w_ret_o"], wts["w_conv_o"]]
    mixed = pl.pallas_call(
        _mix_kernel,
        grid=(t // tm,),
        in_specs=[_rows(tm, D_V), _rows(tm, D_V), _rows(tm, D_V, 1),
                  _rows(tm, D_MODEL, 0), _rows(tm, D_MODEL, 1),
                  _rows(tm, D_CONV)] + [_full(a) for a in big],
        out_specs=_rows(tm, D_MODEL),
        out_shape=jax.ShapeDtypeStruct((t, D_MODEL), BF16),
        scratch_shapes=[pltpu.VMEM((tm, D_V), BF16)],
        compiler_params=_params("parallel"),
        name="mix",
    )(o_f, o_b, vg, gates, gates, uc, *big)
    to = OUT_TM
    return pl.pallas_call(
        _out_kernel,
        grid=(t // to,),
        in_specs=[_rows(to, D_MODEL), _rows(to, D_MODEL), _full(wts["w_out"])],
        out_specs=_rows(to, D_MODEL),
        out_shape=jax.ShapeDtypeStruct((t, D_MODEL), F32),
        compiler_params=_params("parallel"),
        name="out_proj",
    )(mixed, x1, wts["w_out"])


def _rope_tables(seq):
    half = QK_DIM // 2
    inv = ROPE_BASE ** (-jnp.arange(half, dtype=F32) / half)
    ang = jnp.arange(seq, dtype=F32)[:, None] * inv[None, :]
    cos, sin = jnp.cos(ang), jnp.sin(ang)
    return (jnp.concatenate([cos, cos], axis=-1),
            jnp.concatenate([-sin, sin], axis=-1))


def _trunk(x, wts):
    nseq, seq, _ = x.shape
    xf = x.reshape(nseq * seq, D_MODEL)
    x1, h = _ffn(xf, wts["ffn1_norm"], wts["ffn1_w1"], wts["ffn1_w3"], wts["ffn1_w2"],
                 wts["mix_norm"], emit_normed=True, name="ffn1")
    u, qk, vg, gates = _in_proj(h, wts, wts["rope_cos"], wts["rope_sin"], seq)
    uc = _conv_branch(u, wts, seq)
    o_f, o_b = _retention(qk, vg, wts["dec_f"], wts["dec_b"], nseq, seq)
    x2 = _merge(o_f, o_b, vg, gates, uc, x1, wts)
    (y,) = _ffn(x2, wts["ffn2_norm"], wts["ffn2_w1"], wts["ffn2_w3"], wts["ffn2_w2"],
                wts["final_norm"], emit_normed=False, name="ffn2")
    return y.reshape(nseq, seq, D_MODEL)


def _split_cols(w):
    out, lo = [], 0
    for n in FFN_SPLITS:
        out.append(w[:, lo:lo + n].astype(BF16))
        lo += n
    return out


def _split_rows(w):
    out, lo = [], 0
    for n in FFN_SPLITS:
        out.append(w[lo:lo + n, :].astype(BF16))
        lo += n
    return out


def kernel(x_prompt, x_sample, ffn1_norm, ffn1_w1, ffn1_w3, ffn1_w2, mix_norm, w_in, dw_w, dw_b,
           conv_ln_w, conv_ln_b, w_conv_o, decay_fwd, decay_bwd, ret_gn_w, w_ret_o, w_out,
           ffn2_norm, ffn2_w1, ffn2_w3, ffn2_w2, final_norm):
    assert ffn1_w1.shape[0] == 1, "single layer"
    bf = lambda a: a.astype(BF16)
    vec = lambda a: a.reshape(1, -1).astype(F32)
    w_in0 = w_in[0]
    c_qk, c_vg, c_gate = 2 * D_CONV, 2 * D_CONV + 2 * D_QK, 2 * D_CONV + 2 * D_QK + 2 * D_V
    rope_cos, rope_sin = _rope_tables(max(x_prompt.shape[1], x_sample.shape[1]))
    dec = lambda a: jnp.broadcast_to(a[0].astype(F32)[:, None, None], (N_HEADS, 1, V_DIM))
    wts = dict(
        ffn1_norm=vec(ffn1_norm[0]), ffn1_w1=_split_cols(ffn1_w1[0]),
        ffn1_w3=_split_cols(ffn1_w3[0]), ffn1_w2=_split_rows(ffn1_w2[0]),
        mix_norm=vec(mix_norm[0]),
        w_ab=bf(w_in0[:, :c_qk]), w_qk=bf(w_in0[:, c_qk:c_vg]),
        w_vg=bf(w_in0[:, c_vg:c_gate]), w_gates=bf(w_in0[:, c_gate:]),
        rope_cos=rope_cos, rope_sin=rope_sin,
        dw_w=dw_w[0].astype(F32), dw_b=vec(dw_b[0]),
        conv_ln_w=vec(conv_ln_w[0]), conv_ln_b=vec(conv_ln_b[0]),
        w_conv_o=bf(w_conv_o[0]), dec_f=dec(decay_fwd), dec_b=dec(decay_bwd),
        ret_gn_w=vec(ret_gn_w[0]), w_ret_o=bf(w_ret_o[0]), w_out=bf(w_out[0]),
        ffn2_norm=vec(ffn2_norm[0]), ffn2_w1=_split_cols(ffn2_w1[0]),
        ffn2_w3=_split_cols(ffn2_w3[0]), ffn2_w2=_split_rows(ffn2_w2[0]),
        final_norm=vec(final_norm),
    )
    return (_trunk(x_prompt, wts), _trunk(x_sample, wts))
```

```python
import functools

import jax
import jax.numpy as jnp
from jax import lax
from jax.experimental import pallas as pl
from jax.experimental.pallas import tpu as pltpu

D_MODEL = 2048
D_CONV = 1024
CONV_WIDTH = 31
N_HEADS = 8
QK_DIM = 128
V_DIM = 256
D_QK = N_HEADS * QK_DIM
D_V = N_HEADS * V_DIM
CHUNK = 128
RET_CHUNKS = 4
D_FF = 5632
ROPE_BASE = 10000.0
EPS = 1e-6

BF16 = jnp.bfloat16
F32 = jnp.float32

VMEM_LIMIT_BYTES = 56 * 1024 * 1024
SUBLANES = 8
LANES = 128

FFN_SPLITS = (2304, 2304, 1024)
FFN_TM = 512
FFN_LAST_TM = 512
PROJ_TM = 1024
CONV_TS = 128
CONV_HALO = 16
CONV_ROWS = 64
MIX_TM = 512
OUT_TM = 1024

assert sum(FFN_SPLITS) == D_FF


def _params(*sem):
    return pltpu.CompilerParams(dimension_semantics=sem,
                                vmem_limit_bytes=VMEM_LIMIT_BYTES)


def _rms(x, g):
    ms = jnp.mean(x * x, axis=-1, keepdims=True)
    return x * lax.rsqrt(ms + EPS) * g


def _silu(x):
    return x * jax.nn.sigmoid(x)


def _full(a):
    return pl.BlockSpec(a.shape, lambda i: (0,) * a.ndim)


def _rows(tm, width, blk=0):
    return pl.BlockSpec((tm, width), lambda i: (i, blk))


def _swiglu_partial(hn, w1_ref, w3_ref, w2_ref):
    a = jnp.dot(hn, w1_ref[...], preferred_element_type=F32)
    b = jnp.dot(hn, w3_ref[...], preferred_element_type=F32)
    p = (_silu(a) * b).astype(BF16)
    return jnp.dot(p, w2_ref[...], preferred_element_type=F32)


def _ffn_first_kernel(x_ref, g_ref, w1_ref, w3_ref, w2_ref, acc_ref, hn_ref):
    hn = _rms(x_ref[...], g_ref[...]).astype(BF16)
    hn_ref[...] = hn
    acc_ref[...] = _swiglu_partial(hn, w1_ref, w3_ref, w2_ref)


def _ffn_mid_kernel(hn_ref, acc_in_ref, w1_ref, w3_ref, w2_ref, acc_ref):
    acc_ref[...] = acc_in_ref[...] + _swiglu_partial(hn_ref[...], w1_ref, w3_ref, w2_ref)


def _ffn_last_kernel(hn_ref, acc_in_ref, x_ref, g2_ref, w1_ref, w3_ref, w2_ref,
                     y_ref, *h_ref, emit_normed):
    s = acc_in_ref[...] + _swiglu_partial(hn_ref[...], w1_ref, w3_ref, w2_ref)
    y = x_ref[...] + 0.5 * s
    if emit_normed:
        y_ref[...] = y
        h_ref[0][...] = _rms(y, g2_ref[...]).astype(BF16)
    else:
        y_ref[...] = _rms(y, g2_ref[...])


def _ffn(x, g, w1s, w3s, w2s, g2, *, emit_normed, name):
    t = x.shape[0]
    tm = FFN_TM
    assert t % tm == 0 and t % FFN_LAST_TM == 0 and len(w1s) >= 2
    f32_out = jax.ShapeDtypeStruct((t, D_MODEL), F32)
    bf_out = jax.ShapeDtypeStruct((t, D_MODEL), BF16)
    row = _rows(tm, D_MODEL)

    acc, hn = pl.pallas_call(
        _ffn_first_kernel,
        grid=(t // tm,),
        in_specs=[row, _full(g), _full(w1s[0]), _full(w3s[0]), _full(w2s[0])],
        out_specs=[row, row],
        out_shape=[f32_out, bf_out],
        compiler_params=_params("parallel"),
        name=name + "_first",
    )(x, g, w1s[0], w3s[0], w2s[0])

    for w1, w3, w2 in zip(w1s[1:-1], w3s[1:-1], w2s[1:-1]):
        acc = pl.pallas_call(
            _ffn_mid_kernel,
            grid=(t // tm,),
            in_specs=[row, row, _full(w1), _full(w3), _full(w2)],
            out_specs=row,
            out_shape=f32_out,
            compiler_params=_params("parallel"),
            name=name + "_mid",
        )(hn, acc, w1, w3, w2)

    tl = FFN_LAST_TM
    rowl = _rows(tl, D_MODEL)
    return pl.pallas_call(
        functools.partial(_ffn_last_kernel, emit_normed=emit_normed),
        grid=(t // tl,),
        in_specs=[rowl, rowl, rowl, _full(g2), _full(w1s[-1]), _full(w3s[-1]), _full(w2s[-1])],
        out_specs=[rowl, rowl] if emit_normed else [rowl],
        out_shape=[f32_out, bf_out] if emit_normed else [f32_out],
        compiler_params=_params("parallel"),
        name=name + "_last",
    )(hn, acc, x, g2, w1s[-1], w3s[-1], w2s[-1])


def _hdot(h, w_ref, lo, n):
    return jnp.dot(h, w_ref[:, lo:lo + n], preferred_element_type=F32)


def _convqk_proj_kernel(h_ref, wab_ref, wqk_ref, cos_ref, sin_ref, u_ref, qk_ref):
    h = h_ref[...]
    u_ref[...] = _hdot(h, wab_ref, 0, D_CONV) * jax.nn.sigmoid(_hdot(h, wab_ref, D_CONV, D_CONV))
    cos = cos_ref[...]
    sin = sin_ref[...]
    for part, scale in ((0, 1.0), (1, QK_DIM ** -0.5)):
        y = _hdot(h, wqk_ref, part * D_QK, D_QK)
        for hd in range(N_HEADS):
            sl = slice(hd * QK_DIM, (hd + 1) * QK_DIM)
            xh = y[:, sl]
            rot = xh * cos + pltpu.roll(xh, QK_DIM // 2, 1) * sin
            qk_ref[:, part * D_QK + hd * QK_DIM:part * D_QK + (hd + 1) * QK_DIM] = (
                rot * scale).astype(BF16)


def _vg_proj_kernel(h_ref, w_ref, o_ref):
    h = h_ref[...]
    o_ref[:, :D_V] = _hdot(h, w_ref, 0, D_V).astype(BF16)
    o_ref[:, D_V:] = _silu(_hdot(h, w_ref, D_V, D_V)).astype(BF16)


def _gate_proj_kernel(h_ref, w_ref, o_ref):
    h = h_ref[...]
    for lo in range(0, 2 * D_MODEL, D_MODEL):
        o_ref[:, lo:lo + D_MODEL] = jax.nn.sigmoid(_hdot(h, w_ref, lo, D_MODEL)).astype(BF16)


def _in_proj(h, wts, cosf, sinf, seq):
    t = h.shape[0]
    tm = PROJ_TM
    assert t % tm == 0 and seq % tm == 0
    tiles_per_seq = seq // tm
    hrow = _rows(tm, D_MODEL)
    tab = pl.BlockSpec((tm, QK_DIM), lambda i: (i % tiles_per_seq, 0))
    u, qk = pl.pallas_call(
        _convqk_proj_kernel,
        grid=(t // tm,),
        in_specs=[hrow, _full(wts["w_ab"]), _full(wts["w_qk"]), tab, tab],
        out_specs=[_rows(tm, D_CONV), _rows(tm, 2 * D_QK)],
        out_shape=[jax.ShapeDtypeStruct((t, D_CONV), F32),
                   jax.ShapeDtypeStruct((t, 2 * D_QK), BF16)],
        compiler_params=_params("parallel"),
        name="convqk_proj",
    )(h, wts["w_ab"], wts["w_qk"], cosf, sinf)
    wide = []
    for fn, w, name in ((_vg_proj_kernel, wts["w_vg"], "vg_proj"),
                        (_gate_proj_kernel, wts["w_gates"], "gate_proj")):
        wide.append(pl.pallas_call(
            fn,
            grid=(t // tm,),
            in_specs=[hrow, _full(w)],
            out_specs=_rows(tm, w.shape[1]),
            out_shape=jax.ShapeDtypeStruct((t, w.shape[1]), BF16),
            compiler_params=_params("parallel"),
            name=name,
        )(h, w))
    return u, qk, wide[0], wide[1]


def _dwconv_ln_silu(prev_ref, cur_ref, next_ref, w_ref, b_ref, lnw_ref, lnb_ref,
                    pad_ref, y_ref, il, tiles_per_seq):
    ts = cur_ref.shape[0]
    pad_ref[0:CONV_HALO, :] = jnp.where(il > 0, prev_ref[...], 0.0)
    pad_ref[CONV_HALO:CONV_HALO + ts, :] = cur_ref[...]
    pad_ref[CONV_HALO + ts:, :] = jnp.where(il < tiles_per_seq - 1, next_ref[...], 0.0)

    rows = CONV_ROWS
    for r0 in range(0, ts, rows):
        for c0 in range(0, D_CONV, LANES):
            cs = slice(c0, c0 + LANES)
            acc = None
            for b in range(SUBLANES):
                part = None
                for a in range(CONV_WIDTH // SUBLANES + 1):
                    s = SUBLANES * a + b
                    if s < 1 or s > CONV_WIDTH:
                        continue
                    xs = pad_ref[r0 + SUBLANES * a:r0 + SUBLANES * a + rows + SUBLANES, cs]
                    term = xs * w_ref[s - 1:s, cs]
                    part = term if part is None else part + term
                part = part[b:b + rows, :]
                acc = part if acc is None else acc + part
            y_ref[r0:r0 + rows, cs] = acc

    y = y_ref[...] + b_ref[...]
    mu = jnp.mean(y, axis=-1, keepdims=True)
    d = y - mu
    var = jnp.mean(d * d, axis=-1, keepdims=True)
    z = d * lax.rsqrt(var + EPS) * lnw_ref[...] + lnb_ref[...]
    return _silu(z).astype(BF16)


def _conv_kernel(prev_ref, cur_ref, next_ref, w_ref, b_ref, lnw_ref, lnb_ref,
                 o_ref, pad_ref, y_ref, *, tiles_per_seq):
    il = pl.program_id(0) % tiles_per_seq
    o_ref[...] = _dwconv_ln_silu(prev_ref, cur_ref, next_ref, w_ref, b_ref, lnw_ref, lnb_ref,
                                 pad_ref, y_ref, il, tiles_per_seq)


def _conv_branch(u, wts, seq):
    t = u.shape[0]
    ts = CONV_TS
    assert seq % ts == 0 and ts % CONV_HALO == 0 and CONV_HALO > CONV_WIDTH // 2
    assert ts % CONV_ROWS == 0
    hb = ts // CONV_HALO
    last = t // CONV_HALO - 1
    small = [wts["dw_w"], wts["dw_b"], wts["conv_ln_w"], wts["conv_ln_b"]]
    return pl.pallas_call(
        functools.partial(_conv_kernel, tiles_per_seq=seq // ts),
        grid=(t // ts,),
        in_specs=[
            pl.BlockSpec((CONV_HALO, D_CONV), lambda i: (jnp.maximum(i * hb - 1, 0), 0)),
            _rows(ts, D_CONV),
            pl.BlockSpec((CONV_HALO, D_CONV), lambda i: (jnp.minimum((i + 1) * hb, last), 0))]
            + [_full(a) for a in small],
        out_specs=_rows(ts, D_CONV),
        out_shape=jax.ShapeDtypeStruct((t, D_CONV), BF16),
        scratch_shapes=[pltpu.VMEM((ts + 2 * CONV_HALO, D_CONV), F32),
                        pltpu.VMEM((ts, D_CONV), F32)],
        compiler_params=_params("parallel"),
        name="conv_branch",
    )(u, u, u, *small)


def _log_sigmoid(x):
    return -(jnp.maximum(-x, 0.0) + jnp.log1p(jnp.exp(-jnp.abs(x))))


def _ret_kernel(qf_ref, kf_ref, vf_ref, qb_ref, kb_ref, vb_ref, df_ref, db_ref,
                of_ref, ob_ref, dmat_ref, xi_ref, zeta_ref, state_ref):
    c = pl.program_id(1)
    ch = CHUNK

    @pl.when(c == 0)
    def _():
        state_ref[...] = jnp.zeros_like(state_ref)
        row = lax.broadcasted_iota(jnp.int32, (ch, ch), 0).astype(F32)
        col = lax.broadcasted_iota(jnp.int32, (ch, ch), 1).astype(F32)
        rowq = lax.broadcasted_iota(jnp.int32, (ch, QK_DIM), 0).astype(F32)
        for h in range(N_HEADS):
            lg_f = _log_sigmoid(df_ref[h])[:, :ch]
            lg_b = _log_sigmoid(db_ref[h])[:, :ch]
            lq_f = _log_sigmoid(df_ref[h])[:, :QK_DIM]
            lq_b = _log_sigmoid(db_ref[h])[:, :QK_DIM]
            diff_f = row - col
            diff_b = col - row
            dmat_ref[0, h] = jnp.where(diff_f >= 0, jnp.exp(lg_f * jnp.maximum(diff_f, 0.0)), 0.0)
            dmat_ref[1, h] = jnp.where(diff_b > 0, jnp.exp(lg_b * jnp.maximum(diff_b, 0.0)), 0.0)
            xi_ref[0, h] = jnp.exp(lq_f * (rowq + 1.0))
            xi_ref[1, h] = jnp.exp(lq_b * (ch - rowq))
            zeta_ref[0, h] = jnp.exp(lq_f * (ch - 1.0 - rowq))
            zeta_ref[1, h] = jnp.exp(lq_b * rowq)

    dirs = ((0, qf_ref, kf_ref, vf_ref, of_ref, df_ref),
            (1, qb_ref, kb_ref, vb_ref, ob_ref, db_ref))
    work = [(d, h) + tuple(refs) for h in range(N_HEADS) for (d, *refs) in dirs]
    qs = lambda h: slice(h * QK_DIM, (h + 1) * QK_DIM)
    vs = lambda h: slice(h * V_DIM, (h + 1) * V_DIM)

    def rows(d, sub):
        j = sub if d == 0 else RET_CHUNKS - 1 - sub
        return slice(j * ch, (j + 1) * ch)

    scores = {}
    for sub in range(RET_CHUNKS):
        for (d, h, q, k, v, o, dec) in work:
            r = rows(d, sub)
            sc = lax.dot_general(q[r, qs(h)], k[r, qs(h)], (((1,), (1,)), ((), ())),
                                 preferred_element_type=F32)
            scores[sub, d, h] = (sc * dmat_ref[d, h]).astype(BF16)

    for sub in range(RET_CHUNKS):
        for (d, h, q, k, v, o, dec) in work:
            r = rows(d, sub)
            qx = (q[r, qs(h)].astype(F32) * xi_ref[d, h]).astype(BF16)
            lhs = jnp.concatenate([scores[sub, d, h], qx], axis=1)
            rhs = jnp.concatenate([v[r, vs(h)], state_ref[d, h].astype(BF16)], axis=0)
            o[r, vs(h)] = jnp.dot(lhs, rhs, preferred_element_type=F32).astype(BF16)
        for (d, h, q, k, v, o, dec) in work:
            r = rows(d, sub)
            kz = (k[r, qs(h)].astype(F32) * zeta_ref[d, h]).astype(BF16)
            upd = lax.dot_general(kz, v[r, vs(h)], (((0,), (0,)), ((), ())),
                                  preferred_element_type=F32)
            cd = jnp.exp(_log_sigmoid(dec[h]) * float(ch))
            state_ref[d, h] = state_ref[d, h] * cd + upd


def _retention(qk_arr, vg, dec_f, dec_b, nseq, seq):
    t = vg.shape[0]
    ch = CHUNK * RET_CHUNKS
    n = seq // ch
    assert seq % ch == 0 and t == nseq * seq

    def fwd(colblk):
        return lambda b, c: (b * n + c, colblk)

    def bwd(colblk):
        return lambda b, c: (b * n + (n - 1 - c), colblk)

    qk = lambda f, blk: pl.BlockSpec((ch, D_QK), f(blk))
    vv = lambda f: pl.BlockSpec((ch, D_V), f(0))
    dec = pl.BlockSpec((N_HEADS, 1, V_DIM), lambda b, c: (0, 0, 0))
    q_blk, k_blk = 0, 1
    out = jax.ShapeDtypeStruct((t, D_V), BF16)
    return pl.pallas_call(
        _ret_kernel,
        grid=(nseq, n),
        in_specs=[qk(fwd, q_blk), qk(fwd, k_blk), vv(fwd),
                  qk(bwd, q_blk), qk(bwd, k_blk), vv(bwd), dec, dec],
        out_specs=[pl.BlockSpec((ch, D_V), fwd(0)), pl.BlockSpec((ch, D_V), bwd(0))],
        out_shape=[out, out],
        scratch_shapes=[pltpu.VMEM((2, N_HEADS, CHUNK, CHUNK), F32),
                        pltpu.VMEM((2, N_HEADS, CHUNK, QK_DIM), F32),
                        pltpu.VMEM((2, N_HEADS, CHUNK, QK_DIM), F32),
                        pltpu.VMEM((2, N_HEADS, QK_DIM, V_DIM), F32)],
        compiler_params=_params("arbitrary", "arbitrary"),
        name="retention",
    )(qk_arr, qk_arr, vg, qk_arr, qk_arr, vg, dec_f, dec_b)


def _mix_kernel(of_ref, ob_ref, g_ref, gc_ref, gr_ref, uc_ref, gn_ref, wr_ref, wc_ref,
                o_ref, rg_ref):
    yc = gc_ref[...].astype(F32) * jnp.dot(uc_ref[...], wc_ref[...], preferred_element_type=F32)
    for h in range(N_HEADS):
        vs = slice(h * V_DIM, (h + 1) * V_DIM)
        r = of_ref[:, vs].astype(F32) + ob_ref[:, vs].astype(F32)
        mu = jnp.mean(r, axis=-1, keepdims=True)
        d = r - mu
        var = jnp.mean(d * d, axis=-1, keepdims=True)
        rn = d * lax.rsqrt(var + EPS) * gn_ref[:, vs]
        rg_ref[:, vs] = (g_ref[:, vs].astype(F32) * rn).astype(BF16)
    y_ret = jnp.dot(rg_ref[...], wr_ref[...], preferred_element_type=F32)
    mixed = yc + gr_ref[...].astype(F32) * y_ret
    o_ref[...] = mixed.astype(BF16)


def _out_kernel(m_ref, x_ref, wo_ref, o_ref):
    o_ref[...] = x_ref[...] + jnp.dot(m_ref[...], wo_ref[...], preferred_element_type=F32)


def _merge(o_f, o_b, vg, gates, uc, x1, wts):
    t = x1.shape[0]
    tm = MIX_TM
    assert t % tm == 0 and t % OUT_TM == 0
    big = [wts["ret_gn_w"], wts["w_ret_o"], wts["w_conv_o"]]
    mixed = pl.pallas_call(
        _mix_kernel,
        grid=(t // tm,),
        in_specs=[_rows(tm, D_V), _rows(tm, D_V), _rows(tm, D_V, 1),
                  _rows(tm, D_MODEL, 0), _rows(tm, D_MODEL, 1),
                  _rows(tm, D_CONV)] + [_full(a) for a in big],
        out_specs=_rows(tm, D_MODEL),
        out_shape=jax.ShapeDtypeStruct((t, D_MODEL), BF16),
        scratch_shapes=[pltpu.VMEM((tm, D_V), BF16)],
        compiler_params=_params("parallel"),
        name="mix",
    )(o_f, o_b, vg, gates, gates, uc, *big)
    to = OUT_TM
    return pl.pallas_call(
        _out_kernel,
        grid=(t // to,),
        in_specs=[_rows(to, D_MODEL), _rows(to, D_MODEL), _full(wts["w_out"])],
        out_specs=_rows(to, D_MODEL),
        out_shape=jax.ShapeDtypeStruct((t, D_MODEL), F32),
        compiler_params=_params("parallel"),
        name="out_proj",
    )(mixed, x1, wts["w_out"])


def _rope_tables(seq):
    half = QK_DIM // 2
    inv = ROPE_BASE ** (-jnp.arange(half, dtype=F32) / half)
    ang = jnp.arange(seq, dtype=F32)[:, None] * inv[None, :]
    cos, sin = jnp.cos(ang), jnp.sin(ang)
    return (jnp.concatenate([cos, cos], axis=-1),
            jnp.concatenate([-sin, sin], axis=-1))


def _trunk(x, wts):
    nseq, seq, _ = x.shape
    xf = x.reshape(nseq * seq, D_MODEL)
    x1, h = _ffn(xf, wts["ffn1_norm"], wts["ffn1_w1"], wts["ffn1_w3"], wts["ffn1_w2"],
                 wts["mix_norm"], emit_normed=True, name="ffn1")
    u, qk, vg, gates = _in_proj(h, wts, wts["rope_cos"], wts["rope_sin"], seq)
    uc = _conv_branch(u, wts, seq)
    o_f, o_b = _retention(qk, vg, wts["dec_f"], wts["dec_b"], nseq, seq)
    x2 = _merge(o_f, o_b, vg, gates, uc, x1, wts)
    (y,) = _ffn(x2, wts["ffn2_norm"], wts["ffn2_w1"], wts["ffn2_w3"], wts["ffn2_w2"],
                wts["final_norm"], emit_normed=False, name="ffn2")
    return y.reshape(nseq, seq, D_MODEL)


def _split_cols(w):
    out, lo = [], 0
    for n in FFN_SPLITS:
        out.append(w[:, lo:lo + n].astype(BF16))
        lo += n
    return out


def _split_rows(w):
    out, lo = [], 0
    for n in FFN_SPLITS:
        out.append(w[lo:lo + n, :].astype(BF16))
        lo += n
    return out


def kernel(x_prompt, x_sample, ffn1_norm, ffn1_w1, ffn1_w3, ffn1_w2, mix_norm, w_in, dw_w, dw_b,
           conv_ln_w, conv_ln_b, w_conv_o, decay_fwd, decay_bwd, ret_gn_w, w_ret_o, w_out,
           ffn2_norm, ffn2_w1, ffn2_w3, ffn2_w2, final_norm):
    assert ffn1_w1.shape[0] == 1, "single layer"
    bf = lambda a: a.astype(BF16)
    vec = lambda a: a.reshape(1, -1).astype(F32)
    w_in0 = w_in[0]
    c_qk, c_vg, c_gate = 2 * D_CONV, 2 * D_CONV + 2 * D_QK, 2 * D_CONV + 2 * D_QK + 2 * D_V
    rope_cos, rope_sin = _rope_tables(max(x_prompt.shape[1], x_sample.shape[1]))
    dec = lambda a: jnp.broadcast_to(a[0].astype(F32)[:, None, None], (N_HEADS, 1, V_DIM))
    wts = dict(
        ffn1_norm=vec(ffn1_norm[0]), ffn1_w1=_split_cols(ffn1_w1[0]),
        ffn1_w3=_split_cols(ffn1_w3[0]), ffn1_w2=_split_rows(ffn1_w2[0]),
        mix_norm=vec(mix_norm[0]),
        w_ab=bf(w_in0[:, :c_qk]), w_qk=bf(w_in0[:, c_qk:c_vg]),
        w_vg=bf(w_in0[:, c_vg:c_gate]), w_gates=bf(w_in0[:, c_gate:]),
        rope_cos=rope_cos, rope_sin=rope_sin,
        dw_w=dw_w[0].astype(F32), dw_b=vec(dw_b[0]),
        conv_ln_w=vec(conv_ln_w[0]), conv_ln_b=vec(conv_ln_b[0]),
        w_conv_o=bf(w_conv_o[0]), dec_f=dec(decay_fwd), dec_b=dec(decay_bwd),
        ret_gn_w=vec(ret_gn_w[0]), w_ret_o=bf(w_ret_o[0]), w_out=bf(w_out[0]),
        ffn2_norm=vec(ffn2_norm[0]), ffn2_w1=_split_cols(ffn2_w1[0]),
        ffn2_w3=_split_cols(ffn2_w3[0]), ffn2_w2=_split_rows(ffn2_w2[0]),
        final_norm=vec(final_norm),
    )
    return (_trunk(x_prompt, wts), _trunk(x_sample, wts))
```

```python
import functools

import jax
import jax.numpy as jnp
from jax import lax
from jax.experimental import pallas as pl
from jax.experimental.pallas import tpu as pltpu

D_MODEL = 2048
D_CONV = 1024
CONV_WIDTH = 31
N_HEADS = 8
QK_DIM = 128
V_DIM = 256
D_QK = N_HEADS * QK_DIM
D_V = N_HEADS * V_DIM
CHUNK = 128
RET_CHUNKS = 4
D_FF = 5632
ROPE_BASE = 10000.0
ROPE_BLOCK = 128
EPS = 1e-6

BF16 = jnp.bfloat16
F32 = jnp.float32

VMEM_LIMIT_BYTES = 56 * 1024 * 1024
SUBLANES = 8
LANES = 128

FFN_SPLITS = (2304, 2304, 1024)
FFN_TM = 512
FFN_LAST_TM = 512
PROJ_TM = 1024
PROJ_TN = 1024
CONV_TS = 128
CONV_HALO = 16
CONV_ROWS = 64
MIX_TM = 512
OUT_TM = 1024

assert sum(FFN_SPLITS) == D_FF


def _params(*sem):
    return pltpu.CompilerParams(dimension_semantics=sem,
                                vmem_limit_bytes=VMEM_LIMIT_BYTES)


def _rms(x, g):
    ms = jnp.mean(x * x, axis=-1, keepdims=True)
    return x * lax.rsqrt(ms + EPS) * g


def _silu(x):
    return x * jax.nn.sigmoid(x)


def _full(a):
    return pl.BlockSpec(a.shape, lambda i: (0,) * a.ndim)


def _rows(tm, width, blk=0):
    return pl.BlockSpec((tm, width), lambda i: (i, blk))


def _swiglu_partial(hn, w1_ref, w3_ref, w2_ref):
    a = jnp.dot(hn, w1_ref[...], preferred_element_type=F32)
    b = jnp.dot(hn, w3_ref[...], preferred_element_type=F32)
    p = (_silu(a) * b).astype(BF16)
    return jnp.dot(p, w2_ref[...], preferred_element_type=F32)


def _row_halves(ref):
    half = ref.shape[0] // 2
    return (slice(0, half), slice(half, 2 * half))


def _ffn_first_kernel(x_ref, g_ref, w1_ref, w3_ref, w2_ref, acc_ref, hn_ref):
    for rs in _row_halves(x_ref):
        hn = _rms(x_ref[rs, :], g_ref[...]).astype(BF16)
        hn_ref[rs, :] = hn
        acc_ref[rs, :] = _swiglu_partial(hn, w1_ref, w3_ref, w2_ref)


def _ffn_mid_kernel(hn_ref, acc_in_ref, w1_ref, w3_ref, w2_ref, acc_ref):
    acc_ref[...] = acc_in_ref[...] + _swiglu_partial(hn_ref[...], w1_ref, w3_ref, w2_ref)


def _ffn_last_kernel(hn_ref, acc_in_ref, x_ref, g2_ref, w1_ref, w3_ref, w2_ref,
                     y_ref, *h_ref, emit_normed):
    for rs in _row_halves(hn_ref):
        s = acc_in_ref[rs, :] + _swiglu_partial(hn_ref[rs, :], w1_ref, w3_ref, w2_ref)
        y = x_ref[rs, :] + 0.5 * s
        if emit_normed:
            y_ref[rs, :] = y
            h_ref[0][rs, :] = _rms(y, g2_ref[...]).astype(BF16)
        else:
            y_ref[rs, :] = _rms(y, g2_ref[...])


def _ffn(x, g, w1s, w3s, w2s, g2, *, emit_normed, name):
    t = x.shape[0]
    tm = FFN_TM
    assert t % tm == 0 and t % FFN_LAST_TM == 0 and len(w1s) >= 2
    f32_out = jax.ShapeDtypeStruct((t, D_MODEL), F32)
    bf_out = jax.ShapeDtypeStruct((t, D_MODEL), BF16)
    row = _rows(tm, D_MODEL)

    acc, hn = pl.pallas_call(
        _ffn_first_kernel,
        grid=(t // tm,),
        in_specs=[row, _full(g), _full(w1s[0]), _full(w3s[0]), _full(w2s[0])],
        out_specs=[row, row],
        out_shape=[f32_out, bf_out],
        compiler_params=_params("parallel"),
        name=name + "_first",
    )(x, g, w1s[0], w3s[0], w2s[0])

    for w1, w3, w2 in zip(w1s[1:-1], w3s[1:-1], w2s[1:-1]):
        acc = pl.pallas_call(
            _ffn_mid_kernel,
            grid=(t // tm,),
            in_specs=[row, row, _full(w1), _full(w3), _full(w2)],
            out_specs=row,
            out_shape=f32_out,
            compiler_params=_params("parallel"),
            name=name + "_mid",
        )(hn, acc, w1, w3, w2)

    tl = FFN_LAST_TM
    rowl = _rows(tl, D_MODEL)
    return pl.pallas_call(
        functools.partial(_ffn_last_kernel, emit_normed=emit_normed),
        grid=(t // tl,),
        in_specs=[rowl, rowl, rowl, _full(g2), _full(w1s[-1]), _full(w3s[-1]), _full(w2s[-1])],
        out_specs=[rowl, rowl] if emit_normed else [rowl],
        out_shape=[f32_out, bf_out] if emit_normed else [f32_out],
        compiler_params=_params("parallel"),
        name=name + "_last",
    )(hn, acc, x, g2, w1s[-1], w3s[-1], w2s[-1])


def _hdot(h, w_ref, lo, n):
    return jnp.dot(h, w_ref[:, lo:lo + n], preferred_element_type=F32)


def _convqk_proj_kernel(h_ref, wab_ref, wqk_ref, cos_ref, sin_ref, u_ref, qk_ref):
    h = h_ref[...]
    u_ref[...] = _hdot(h, wab_ref, 0, D_CONV) * jax.nn.sigmoid(_hdot(h, wab_ref, D_CONV, D_CONV))
    cos = cos_ref[...]
    sin = sin_ref[...]
    for part, scale in ((0, 1.0), (1, QK_DIM ** -0.5)):
        y = _hdot(h, wqk_ref, part * D_QK, D_QK)
        for hd in range(N_HEADS):
            sl = slice(hd * QK_DIM, (hd + 1) * QK_DIM)
            xh = y[:, sl]
            rot = xh * cos + pltpu.roll(xh, QK_DIM // 2, 1) * sin
            qk_ref[:, part * D_QK + hd * QK_DIM:part * D_QK + (hd + 1) * QK_DIM] = (
                rot * scale).astype(BF16)


def _vg_proj_kernel(h_ref, w_ref, o_ref):
    h = h_ref[...]
    for lo in range(0, 2 * D_V, PROJ_TN):
        y = _hdot(h, w_ref, lo, PROJ_TN)
        o_ref[:, lo:lo + PROJ_TN] = (y if lo < D_V else _silu(y)).astype(BF16)


def _gate_proj_kernel(h_ref, w_ref, o_ref):
    h = h_ref[...]
    for lo in range(0, 2 * D_MODEL, PROJ_TN):
        o_ref[:, lo:lo + PROJ_TN] = jax.nn.sigmoid(_hdot(h, w_ref, lo, PROJ_TN)).astype(BF16)


def _in_proj(h, wts, cosf, sinf, seq):
    t = h.shape[0]
    tm = PROJ_TM
    assert t % tm == 0 and seq % tm == 0
    tiles_per_seq = seq // tm
    hrow = _rows(tm, D_MODEL)
    tab = pl.BlockSpec((tm, QK_DIM), lambda i: (i % tiles_per_seq, 0))
    u, qk = pl.pallas_call(
        _convqk_proj_kernel,
        grid=(t // tm,),
        in_specs=[hrow, _full(wts["w_ab"]), _full(wts["w_qk"]), tab, tab],
        out_specs=[_rows(tm, D_CONV), _rows(tm, 2 * D_QK)],
        out_shape=[jax.ShapeDtypeStruct((t, D_CONV), F32),
                   jax.ShapeDtypeStruct((t, 2 * D_QK), BF16)],
        compiler_params=_params("parallel"),
        name="convqk_proj",
    )(h, wts["w_ab"], wts["w_qk"], cosf, sinf)
    wide = []
    for fn, w, name in ((_vg_proj_kernel, wts["w_vg"], "vg_proj"),
                        (_gate_proj_kernel, wts["w_gates"], "gate_proj")):
        wide.append(pl.pallas_call(
            fn,
            grid=(t // tm,),
            in_specs=[hrow, _full(w)],
            out_specs=_rows(tm, w.shape[1]),
            out_shape=jax.ShapeDtypeStruct((t, w.shape[1]), BF16),
            compiler_params=_params("parallel"),
            name=name,
        )(h, w))
    return u, qk, wide[0], wide[1]


def _dwconv_ln_silu(prev_ref, cur_ref, next_ref, w_ref, b_ref, lnw_ref, lnb_ref,
                    pad_ref, y_ref, il, tiles_per_seq):
    ts = cur_ref.shape[0]
    pad_ref[0:CONV_HALO, :] = jnp.where(il > 0, prev_ref[...], 0.0)
    pad_ref[CONV_HALO:CONV_HALO + ts, :] = cur_ref[...]
    pad_ref[CONV_HALO + ts:, :] = jnp.where(il < tiles_per_seq - 1, next_ref[...], 0.0)

    rows = CONV_ROWS
    for r0 in range(0, ts, rows):
        for c0 in range(0, D_CONV, LANES):
            cs = slice(c0, c0 + LANES)
            acc = None
            for b in range(SUBLANES):
                part = None
                for a in range(CONV_WIDTH // SUBLANES + 1):
                    s = SUBLANES * a + b
                    if s < 1 or s > CONV_WIDTH:
                        continue
                    xs = pad_ref[r0 + SUBLANES * a:r0 + SUBLANES * a + rows + SUBLANES, cs]
                    term = xs * w_ref[s - 1:s, cs]
                    part = term if part is None else part + term
                part = part[b:b + rows, :]
                acc = part if acc is None else acc + part
            y_ref[r0:r0 + rows, cs] = acc

    y = y_ref[...] + b_ref[...]
    mu = jnp.mean(y, axis=-1, keepdims=True)
    d = y - mu
    var = jnp.mean(d * d, axis=-1, keepdims=True)
    z = d * lax.rsqrt(var + EPS) * lnw_ref[...] + lnb_ref[...]
    return _silu(z).astype(BF16)


def _conv_kernel(prev_ref, cur_ref, next_ref, w_ref, b_ref, lnw_ref, lnb_ref,
                 o_ref, pad_ref, y_ref, *, tiles_per_seq):
    il = pl.program_id(0) % tiles_per_seq
    o_ref[...] = _dwconv_ln_silu(prev_ref, cur_ref, next_ref, w_ref, b_ref, lnw_ref, lnb_ref,
                                 pad_ref, y_ref, il, tiles_per_seq)


def _conv_branch(u, wts, seq):
    t = u.shape[0]
    ts = CONV_TS
    assert seq % ts == 0 and ts % CONV_HALO == 0 and CONV_HALO > CONV_WIDTH // 2
    assert ts % CONV_ROWS == 0
    hb = ts // CONV_HALO
    last = t // CONV_HALO - 1
    small = [wts["dw_w"], wts["dw_b"], wts["conv_ln_w"], wts["conv_ln_b"]]
    return pl.pallas_call(
        functools.partial(_conv_kernel, tiles_per_seq=seq // ts),
        grid=(t // ts,),
        in_specs=[
            pl.BlockSpec((CONV_HALO, D_CONV), lambda i: (jnp.maximum(i * hb - 1, 0), 0)),
            _rows(ts, D_CONV),
            pl.BlockSpec((CONV_HALO, D_CONV), lambda i: (jnp.minimum((i + 1) * hb, last), 0))]
            + [_full(a) for a in small],
        out_specs=_rows(ts, D_CONV),
        out_shape=jax.ShapeDtypeStruct((t, D_CONV), BF16),
        scratch_shapes=[pltpu.VMEM((ts + 2 * CONV_HALO, D_CONV), F32),
                        pltpu.VMEM((ts, D_CONV), F32)],
        compiler_params=_params("parallel"),
        name="conv_branch",
    )(u, u, u, *small)


def _log_sigmoid(x):
    return -(jnp.maximum(-x, 0.0) + jnp.log1p(jnp.exp(-jnp.abs(x))))


def _ret_kernel(qf_ref, kf_ref, vf_ref, qb_ref, kb_ref, vb_ref, df_ref, db_ref,
                of_ref, ob_ref, dmat_ref, xi_ref, zeta_ref, state_ref):
    c = pl.program_id(1)
    ch = CHUNK

    @pl.when(c == 0)
    def _():
        state_ref[...] = jnp.zeros_like(state_ref)
        row = lax.broadcasted_iota(jnp.int32, (ch, ch), 0).astype(F32)
        col = lax.broadcasted_iota(jnp.int32, (ch, ch), 1).astype(F32)
        rowq = lax.broadcasted_iota(jnp.int32, (ch, QK_DIM), 0).astype(F32)
        for h in range(N_HEADS):
            lg_f = _log_sigmoid(df_ref[h])[:, :ch]
            lg_b = _log_sigmoid(db_ref[h])[:, :ch]
            lq_f = _log_sigmoid(df_ref[h])[:, :QK_DIM]
            lq_b = _log_sigmoid(db_ref[h])[:, :QK_DIM]
            diff_f = row - col
            diff_b = col - row
            dmat_ref[0, h] = jnp.where(diff_f >= 0, jnp.exp(lg_f * jnp.maximum(diff_f, 0.0)), 0.0)
            dmat_ref[1, h] = jnp.where(diff_b > 0, jnp.exp(lg_b * jnp.maximum(diff_b, 0.0)), 0.0)
            xi_ref[0, h] = jnp.exp(lq_f * (rowq + 1.0))
            xi_ref[1, h] = jnp.exp(lq_b * (ch - rowq))
            zeta_ref[0, h] = jnp.exp(lq_f * (ch - 1.0 - rowq))
            zeta_ref[1, h] = jnp.exp(lq_b * rowq)

    dirs = ((0, qf_ref, kf_ref, vf_ref, of_ref, df_ref),
            (1, qb_ref, kb_ref, vb_ref, ob_ref, db_ref))
    work = [(d, h) + tuple(refs) for h in range(N_HEADS) for (d, *refs) in dirs]
    qs = lambda h: slice(h * QK_DIM, (h + 1) * QK_DIM)
    vs = lambda h: slice(h * V_DIM, (h + 1) * V_DIM)

    def rows(d, sub):
        j = sub if d == 0 else RET_CHUNKS - 1 - sub
        return slice(j * ch, (j + 1) * ch)

    scores = {}
    for sub in range(RET_CHUNKS):
        for (d, h, q, k, v, o, dec) in work:
            r = rows(d, sub)
            sc = lax.dot_general(q[r, qs(h)], k[r, qs(h)], (((1,), (1,)), ((), ())),
                                 preferred_element_type=F32)
            scores[sub, d, h] = (sc * dmat_ref[d, h]).astype(BF16)

    for sub in range(RET_CHUNKS):
        for (d, h, q, k, v, o, dec) in work:
            r = rows(d, sub)
            qx = (q[r, qs(h)].astype(F32) * xi_ref[d, h]).astype(BF16)
            lhs = jnp.concatenate([scores[sub, d, h], qx], axis=1)
            rhs = jnp.concatenate([v[r, vs(h)], state_ref[d, h].astype(BF16)], axis=0)
            o[r, vs(h)] = jnp.dot(lhs, rhs, preferred_element_type=F32).astype(BF16)
        for (d, h, q, k, v, o, dec) in work:
            r = rows(d, sub)
            kz = (k[r, qs(h)].astype(F32) * zeta_ref[d, h]).astype(BF16)
            upd = lax.dot_general(kz, v[r, vs(h)], (((0,), (0,)), ((), ())),
                                  preferred_element_type=F32)
            cd = jnp.exp(_log_sigmoid(dec[h]) * float(ch))
            state_ref[d, h] = state_ref[d, h] * cd + upd


def _retention(qk_arr, vg, dec_f, dec_b, nseq, seq):
    t = vg.shape[0]
    ch = CHUNK * RET_CHUNKS
    n = seq // ch
    assert seq % ch == 0 and t == nseq * seq

    def fwd(colblk):
        return lambda b, c: (b * n + c, colblk)

    def bwd(colblk):
        return lambda b, c: (b * n + (n - 1 - c), colblk)

    qk = lambda f, blk: pl.BlockSpec((ch, D_QK), f(blk))
    vv = lambda f: pl.BlockSpec((ch, D_V), f(0))
    dec = pl.BlockSpec((N_HEADS, 1, V_DIM), lambda b, c: (0, 0, 0))
    q_blk, k_blk = 0, 1
    out = jax.ShapeDtypeStruct((t, D_V), BF16)
    return pl.pallas_call(
        _ret_kernel,
        grid=(nseq, n),
        in_specs=[qk(fwd, q_blk), qk(fwd, k_blk), vv(fwd),
                  qk(bwd, q_blk), qk(bwd, k_blk), vv(bwd), dec, dec],
        out_specs=[pl.BlockSpec((ch, D_V), fwd(0)), pl.BlockSpec((ch, D_V), bwd(0))],
        out_shape=[out, out],
        scratch_shapes=[pltpu.VMEM((2, N_HEADS, CHUNK, CHUNK), F32),
                        pltpu.VMEM((2, N_HEADS, CHUNK, QK_DIM), F32),
                        pltpu.VMEM((2, N_HEADS, CHUNK, QK_DIM), F32),
                        pltpu.VMEM((2, N_HEADS, QK_DIM, V_DIM), F32)],
        compiler_params=_params("arbitrary", "arbitrary"),
        name="retention",
    )(qk_arr, qk_arr, vg, qk_arr, qk_arr, vg, dec_f, dec_b)


def _mix_kernel(of_ref, ob_ref, g_ref, gc_ref, gr_ref, uc_ref, gn_ref, wr_ref, wc_ref,
                o_ref, rg_ref):
    yc = gc_ref[...].astype(F32) * jnp.dot(uc_ref[...], wc_ref[...], preferred_element_type=F32)
    for h in range(N_HEADS):
        vs = slice(h * V_DIM, (h + 1) * V_DIM)
        r = of_ref[:, vs].astype(F32) + ob_ref[:, vs].astype(F32)
        mu = jnp.mean(r, axis=-1, keepdims=True)
        d = r - mu
        var = jnp.mean(d * d, axis=-1, keepdims=True)
        rn = d * lax.rsqrt(var + EPS) * gn_ref[:, vs]
        rg_ref[:, vs] = (g_ref[:, vs].astype(F32) * rn).astype(BF16)
    y_ret = jnp.dot(rg_ref[...], wr_ref[...], preferred_element_type=F32)
    mixed = yc + gr_ref[...].astype(F32) * y_ret
    o_ref[...] = mixed.astype(BF16)


def _out_kernel(m_ref, x_ref, wo_ref, o_ref):
    o_ref[...] = x_ref[...] + jnp.dot(m_ref[...], wo_ref[...], preferred_element_type=F32)


def _merge(o_f, o_b, vg, gates, uc, x1, wts):
    t = x1.shape[0]
    tm = MIX_TM
    assert t % tm == 0 and t % OUT_TM == 0
    big = [wts["ret_gn_w"], wts["w_ret_o"], wts["w_conv_o"]]
    mixed = pl.pallas_call(
        _mix_kernel,
        grid=(t // tm,),
        in_specs=[_rows(tm, D_V), _rows(tm, D_V), _rows(tm, D_V, 1),
                  _rows(tm, D_MODEL, 0), _rows(tm, D_MODEL, 1),
                  _rows(tm, D_CONV)] + [_full(a) for a in big],
        out_specs=_rows(tm, D_MODEL),
        out_shape=jax.ShapeDtypeStruct((t, D_MODEL), BF16),
        scratch_shapes=[pltpu.VMEM((tm, D_V), BF16)],
        compiler_params=_params("parallel"),
        name="mix",
    )(o_f, o_b, vg, gates, gates, uc, *big)
    to = OUT_TM
    return pl.pallas_call(
        _out_kernel,
        grid=(t // to,),
        in_specs=[_rows(to, D_MODEL), _rows(to, D_MODEL), _full(wts["w_out"])],
        out_specs=_rows(to, D_MODEL),
        out_shape=jax.ShapeDtypeStruct((t, D_MODEL), F32),
        compiler_params=_params("parallel"),
        name="out_proj",
    )(mixed, x1, wts["w_out"])


def _rope_tables(seq):
    half = QK_DIM // 2
    assert seq % ROPE_BLOCK == 0
    inv = ROPE_BASE ** (-jnp.arange(half, dtype=F32) / half)
    coarse = (jnp.arange(seq // ROPE_BLOCK, dtype=F32) * ROPE_BLOCK)[:, None] * inv[None, :]
    fine = jnp.arange(ROPE_BLOCK, dtype=F32)[:, None] * inv[None, :]
    ca, sa = jnp.cos(coarse)[:, None, :], jnp.sin(coarse)[:, None, :]
    cb, sb = jnp.cos(fine)[None, :, :], jnp.sin(fine)[None, :, :]
    cos = (ca * cb - sa * sb).reshape(seq, half)
    sin = (sa * cb + ca * sb).reshape(seq, half)
    return (jnp.concatenate([cos, cos], axis=-1),
            jnp.concatenate([-sin, sin], axis=-1))


def _trunk(x, wts):
    nseq, seq, _ = x.shape
    xf = x.reshape(nseq * seq, D_MODEL)
    x1, h = _ffn(xf, wts["ffn1_norm"], wts["ffn1_w1"], wts["ffn1_w3"], wts["ffn1_w2"],
                 wts["mix_norm"], emit_normed=True, name="ffn1")
    u, qk, vg, gates = _in_proj(h, wts, wts["rope_cos"], wts["rope_sin"], seq)
    uc = _conv_branch(u, wts, seq)
    o_f, o_b = _retention(qk, vg, wts["dec_f"], wts["dec_b"], nseq, seq)
    x2 = _merge(o_f, o_b, vg, gates, uc, x1, wts)
    (y,) = _ffn(x2, wts["ffn2_norm"], wts["ffn2_w1"], wts["ffn2_w3"], wts["ffn2_w2"],
                wts["final_norm"], emit_normed=False, name="ffn2")
    return y.reshape(nseq, seq, D_MODEL)


def _split_cols(w):
    out, lo = [], 0
    for n in FFN_SPLITS:
        out.append(w[:, lo:lo + n].astype(BF16))
        lo += n
    return out


def _split_rows(w):
    out, lo = [], 0
    for n in FFN_SPLITS:
        out.append(w[lo:lo + n, :].astype(BF16))
        lo += n
    return out


def kernel(x_prompt, x_sample, ffn1_norm, ffn1_w1, ffn1_w3, ffn1_w2, mix_norm, w_in, dw_w, dw_b,
           conv_ln_w, conv_ln_b, w_conv_o, decay_fwd, decay_bwd, ret_gn_w, w_ret_o, w_out,
           ffn2_norm, ffn2_w1, ffn2_w3, ffn2_w2, final_norm):
    assert ffn1_w1.shape[0] == 1, "single layer"
    bf = lambda a: a.astype(BF16)
    vec = lambda a: a.reshape(1, -1).astype(F32)
    w_in0 = w_in[0]
    c_qk, c_vg, c_gate = 2 * D_CONV, 2 * D_CONV + 2 * D_QK, 2 * D_CONV + 2 * D_QK + 2 * D_V
    rope_cos, rope_sin = _rope_tables(max(x_prompt.shape[1], x_sample.shape[1]))
    dec = lambda a: jnp.broadcast_to(a[0].astype(F32)[:, None, None], (N_HEADS, 1, V_DIM))
    wts = dict(
        ffn1_norm=vec(ffn1_norm[0]), ffn1_w1=_split_cols(ffn1_w1[0]),
        ffn1_w3=_split_cols(ffn1_w3[0]), ffn1_w2=_split_rows(ffn1_w2[0]),
        mix_norm=vec(mix_norm[0]),
        w_ab=bf(w_in0[:, :c_qk]), w_qk=bf(w_in0[:, c_qk:c_vg]),
        w_vg=bf(w_in0[:, c_vg:c_gate]), w_gates=bf(w_in0[:, c_gate:]),
        rope_cos=rope_cos, rope_sin=rope_sin,
        dw_w=dw_w[0].astype(F32), dw_b=vec(dw_b[0]),
        conv_ln_w=vec(conv_ln_w[0]), conv_ln_b=vec(conv_ln_b[0]),
        w_conv_o=bf(w_conv_o[0]), dec_f=dec(decay_fwd), dec_b=dec(decay_bwd),
        ret_gn_w=vec(ret_gn_w[0]), w_ret_o=bf(w_ret_o[0]), w_out=bf(w_out[0]),
        ffn2_norm=vec(ffn2_norm[0]), ffn2_w1=_split_cols(ffn2_w1[0]),
        ffn2_w3=_split_cols(ffn2_w3[0]), ffn2_w2=_split_rows(ffn2_w2[0]),
        final_norm=vec(final_norm),
    )
    return (_trunk(x_prompt, wts), _trunk(x_sample, wts))
```

```python
import functools

import jax
import jax.numpy as jnp
from jax import lax
from jax.experimental import pallas as pl
from jax.experimental.pallas import tpu as pltpu

D_MODEL = 2048
D_CONV = 1024
CONV_WIDTH = 31
N_HEADS = 8
QK_DIM = 128
V_DIM = 256
D_QK = N_HEADS * QK_DIM
D_V = N_HEADS * V_DIM
CHUNK = 128
RET_CHUNKS = 4
D_FF = 5632
ROPE_BASE = 10000.0
ROPE_BLOCK = 128
EPS = 1e-6

BF16 = jnp.bfloat16
F32 = jnp.float32

VMEM_LIMIT_BYTES = 56 * 1024 * 1024
SUBLANES = 8
LANES = 128

FFN_SPLITS = (2304, 2304, 1024)
FFN_TM = 512
FFN_LAST_TM = 512
PROJ_TM = 1024
PROJ_TN = 1024
CONV_TS = 128
CONV_HALO = 16
CONV_ROWS = 64
MIX_TM = 512
CAST_ROWS = 256
OUT_TM = 1024

assert sum(FFN_SPLITS) == D_FF


def _params(*sem):
    return pltpu.CompilerParams(dimension_semantics=sem,
                                vmem_limit_bytes=VMEM_LIMIT_BYTES)


def _rms(x, g):
    ms = jnp.mean(x * x, axis=-1, keepdims=True)
    return x * lax.rsqrt(ms + EPS) * g


def _silu(x):
    return x * jax.nn.sigmoid(x)


def _full(a):
    return pl.BlockSpec(a.shape, lambda i: (0,) * a.ndim)


def _rows(tm, width, blk=0):
    return pl.BlockSpec((tm, width), lambda i: (i, blk))


def _swiglu_partial(hn, w1_ref, w3_ref, w2_ref):
    a = jnp.dot(hn, w1_ref[...], preferred_element_type=F32)
    b = jnp.dot(hn, w3_ref[...], preferred_element_type=F32)
    p = (_silu(a) * b).astype(BF16)
    return jnp.dot(p, w2_ref[...], preferred_element_type=F32)


def _row_halves(ref):
    half = ref.shape[0] // 2
    return (slice(0, half), slice(half, 2 * half))


def _ffn_first_kernel(x_ref, g_ref, w1_ref, w3_ref, w2_ref, acc_ref, hn_ref):
    for rs in _row_halves(x_ref):
        hn = _rms(x_ref[rs, :], g_ref[...]).astype(BF16)
        hn_ref[rs, :] = hn
        acc_ref[rs, :] = _swiglu_partial(hn, w1_ref, w3_ref, w2_ref)


def _ffn_mid_kernel(hn_ref, acc_in_ref, w1_ref, w3_ref, w2_ref, acc_ref):
    acc_ref[...] = acc_in_ref[...] + _swiglu_partial(hn_ref[...], w1_ref, w3_ref, w2_ref)


def _ffn_last_kernel(hn_ref, acc_in_ref, x_ref, g2_ref, w1_ref, w3_ref, w2_ref,
                     y_ref, *h_ref, emit_normed):
    for rs in _row_halves(hn_ref):
        s = acc_in_ref[rs, :] + _swiglu_partial(hn_ref[rs, :], w1_ref, w3_ref, w2_ref)
        y = x_ref[rs, :] + 0.5 * s
        if emit_normed:
            y_ref[rs, :] = y
            h_ref[0][rs, :] = _rms(y, g2_ref[...]).astype(BF16)
        else:
            y_ref[rs, :] = _rms(y, g2_ref[...])


def _ffn(x, g, w1s, w3s, w2s, g2, *, emit_normed, name):
    t = x.shape[0]
    tm = FFN_TM
    assert t % tm == 0 and t % FFN_LAST_TM == 0 and len(w1s) >= 2
    f32_out = jax.ShapeDtypeStruct((t, D_MODEL), F32)
    bf_out = jax.ShapeDtypeStruct((t, D_MODEL), BF16)
    row = _rows(tm, D_MODEL)

    acc, hn = pl.pallas_call(
        _ffn_first_kernel,
        grid=(t // tm,),
        in_specs=[row, _full(g), _full(w1s[0]), _full(w3s[0]), _full(w2s[0])],
        out_specs=[row, row],
        out_shape=[f32_out, bf_out],
        compiler_params=_params("parallel"),
        name=name + "_first",
    )(x, g, w1s[0], w3s[0], w2s[0])

    for w1, w3, w2 in zip(w1s[1:-1], w3s[1:-1], w2s[1:-1]):
        acc = pl.pallas_call(
            _ffn_mid_kernel,
            grid=(t // tm,),
            in_specs=[row, row, _full(w1), _full(w3), _full(w2)],
            out_specs=row,
            out_shape=f32_out,
            compiler_params=_params("parallel"),
            name=name + "_mid",
        )(hn, acc, w1, w3, w2)

    tl = FFN_LAST_TM
    rowl = _rows(tl, D_MODEL)
    return pl.pallas_call(
        functools.partial(_ffn_last_kernel, emit_normed=emit_normed),
        grid=(t // tl,),
        in_specs=[rowl, rowl, rowl, _full(g2), _full(w1s[-1]), _full(w3s[-1]), _full(w2s[-1])],
        out_specs=[rowl, rowl] if emit_normed else [rowl],
        out_shape=[f32_out, bf_out] if emit_normed else [f32_out],
        compiler_params=_params("parallel"),
        name=name + "_last",
    )(hn, acc, x, g2, w1s[-1], w3s[-1], w2s[-1])


def _hdot(h, w_ref, lo, n):
    return jnp.dot(h, w_ref[:, lo:lo + n], preferred_element_type=F32)


def _convqk_proj_kernel(h_ref, wab_ref, wqk_ref, cos_ref, sin_ref, u_ref, qk_ref):
    h = h_ref[...]
    u_ref[...] = _hdot(h, wab_ref, 0, D_CONV) * jax.nn.sigmoid(_hdot(h, wab_ref, D_CONV, D_CONV))
    cos = cos_ref[...]
    sin = sin_ref[...]
    for part, scale in ((0, 1.0), (1, QK_DIM ** -0.5)):
        y = _hdot(h, wqk_ref, part * D_QK, D_QK)
        for hd in range(N_HEADS):
            sl = slice(hd * QK_DIM, (hd + 1) * QK_DIM)
            xh = y[:, sl]
            rot = xh * cos + pltpu.roll(xh, QK_DIM // 2, 1) * sin
            qk_ref[:, part * D_QK + hd * QK_DIM:part * D_QK + (hd + 1) * QK_DIM] = (
                rot * scale).astype(BF16)


def _vg_proj_kernel(h_ref, w_ref, o_ref):
    h = h_ref[...]
    for lo in range(0, 2 * D_V, PROJ_TN):
        y = _hdot(h, w_ref, lo, PROJ_TN)
        o_ref[:, lo:lo + PROJ_TN] = (y if lo < D_V else _silu(y)).astype(BF16)


def _gate_proj_kernel(h_ref, w_ref, o_ref):
    h = h_ref[...]
    for lo in range(0, 2 * D_MODEL, PROJ_TN):
        o_ref[:, lo:lo + PROJ_TN] = jax.nn.sigmoid(_hdot(h, w_ref, lo, PROJ_TN)).astype(BF16)


def _in_proj(h, wts, cosf, sinf, seq):
    t = h.shape[0]
    tm = PROJ_TM
    assert t % tm == 0 and seq % tm == 0
    tiles_per_seq = seq // tm
    hrow = _rows(tm, D_MODEL)
    tab = pl.BlockSpec((tm, QK_DIM), lambda i: (i % tiles_per_seq, 0))
    u, qk = pl.pallas_call(
        _convqk_proj_kernel,
        grid=(t // tm,),
        in_specs=[hrow, _full(wts["w_ab"]), _full(wts["w_qk"]), tab, tab],
        out_specs=[_rows(tm, D_CONV), _rows(tm, 2 * D_QK)],
        out_shape=[jax.ShapeDtypeStruct((t, D_CONV), F32),
                   jax.ShapeDtypeStruct((t, 2 * D_QK), BF16)],
        compiler_params=_params("parallel"),
        name="convqk_proj",
    )(h, wts["w_ab"], wts["w_qk"], cosf, sinf)
    wide = []
    for fn, w, name in ((_vg_proj_kernel, wts["w_vg"], "vg_proj"),
                        (_gate_proj_kernel, wts["w_gates"], "gate_proj")):
        wide.append(pl.pallas_call(
            fn,
            grid=(t // tm,),
            in_specs=[hrow, _full(w)],
            out_specs=_rows(tm, w.shape[1]),
            out_shape=jax.ShapeDtypeStruct((t, w.shape[1]), BF16),
            compiler_params=_params("parallel"),
            name=name,
        )(h, w))
    return u, qk, wide[0], wide[1]


def _dwconv_ln_silu(prev_ref, cur_ref, next_ref, w_ref, b_ref, lnw_ref, lnb_ref,
                    pad_ref, y_ref, il, tiles_per_seq):
    ts = cur_ref.shape[0]
    pad_ref[0:CONV_HALO, :] = jnp.where(il > 0, prev_ref[...], 0.0)
    pad_ref[CONV_HALO:CONV_HALO + ts, :] = cur_ref[...]
    pad_ref[CONV_HALO + ts:, :] = jnp.where(il < tiles_per_seq - 1, next_ref[...], 0.0)

    rows = CONV_ROWS
    for r0 in range(0, ts, rows):
        for c0 in range(0, D_CONV, LANES):
            cs = slice(c0, c0 + LANES)
            acc = None
            for b in range(SUBLANES):
                part = None
                for a in range(CONV_WIDTH // SUBLANES + 1):
                    s = SUBLANES * a + b
                    if s < 1 or s > CONV_WIDTH:
                        continue
                    xs = pad_ref[r0 + SUBLANES * a:r0 + SUBLANES * a + rows + SUBLANES, cs]
                    term = xs * w_ref[s - 1:s, cs]
                    part = term if part is None else part + term
                part = part[b:b + rows, :]
                acc = part if acc is None else acc + part
            y_ref[r0:r0 + rows, cs] = acc

    y = y_ref[...] + b_ref[...]
    mu = jnp.mean(y, axis=-1, keepdims=True)
    d = y - mu
    var = jnp.mean(d * d, axis=-1, keepdims=True)
    z = d * lax.rsqrt(var + EPS) * lnw_ref[...] + lnb_ref[...]
    return _silu(z).astype(BF16)


def _conv_kernel(prev_ref, cur_ref, next_ref, w_ref, b_ref, lnw_ref, lnb_ref,
                 o_ref, pad_ref, y_ref, *, tiles_per_seq):
    il = pl.program_id(0) % tiles_per_seq
    o_ref[...] = _dwconv_ln_silu(prev_ref, cur_ref, next_ref, w_ref, b_ref, lnw_ref, lnb_ref,
                                 pad_ref, y_ref, il, tiles_per_seq)


def _conv_branch(u, wts, seq):
    t = u.shape[0]
    ts = CONV_TS
    assert seq % ts == 0 and ts % CONV_HALO == 0 and CONV_HALO > CONV_WIDTH // 2
    assert ts % CONV_ROWS == 0
    hb = ts // CONV_HALO
    last = t // CONV_HALO - 1
    small = [wts["dw_w"], wts["dw_b"], wts["conv_ln_w"], wts["conv_ln_b"]]
    return pl.pallas_call(
        functools.partial(_conv_kernel, tiles_per_seq=seq // ts),
        grid=(t // ts,),
        in_specs=[
            pl.BlockSpec((CONV_HALO, D_CONV), lambda i: (jnp.maximum(i * hb - 1, 0), 0)),
            _rows(ts, D_CONV),
            pl.BlockSpec((CONV_HALO, D_CONV), lambda i: (jnp.minimum((i + 1) * hb, last), 0))]
            + [_full(a) for a in small],
        out_specs=_rows(ts, D_CONV),
        out_shape=jax.ShapeDtypeStruct((t, D_CONV), BF16),
        scratch_shapes=[pltpu.VMEM((ts + 2 * CONV_HALO, D_CONV), F32),
                        pltpu.VMEM((ts, D_CONV), F32)],
        compiler_params=_params("parallel"),
        name="conv_branch",
    )(u, u, u, *small)


def _log_sigmoid(x):
    return -(jnp.maximum(-x, 0.0) + jnp.log1p(jnp.exp(-jnp.abs(x))))


def _ret_kernel(qf_ref, kf_ref, vf_ref, qb_ref, kb_ref, vb_ref, df_ref, db_ref,
                of_ref, ob_ref, dmat_ref, xi_ref, zeta_ref, state_ref):
    c = pl.program_id(1)
    ch = CHUNK

    @pl.when(c == 0)
    def _():
        state_ref[...] = jnp.zeros_like(state_ref)
        row = lax.broadcasted_iota(jnp.int32, (ch, ch), 0).astype(F32)
        col = lax.broadcasted_iota(jnp.int32, (ch, ch), 1).astype(F32)
        rowq = lax.broadcasted_iota(jnp.int32, (ch, QK_DIM), 0).astype(F32)
        for h in range(N_HEADS):
            lg_f = _log_sigmoid(df_ref[h])[:, :ch]
            lg_b = _log_sigmoid(db_ref[h])[:, :ch]
            lq_f = _log_sigmoid(df_ref[h])[:, :QK_DIM]
            lq_b = _log_sigmoid(db_ref[h])[:, :QK_DIM]
            diff_f = row - col
            diff_b = col - row
            dmat_ref[0, h] = jnp.where(diff_f >= 0, jnp.exp(lg_f * jnp.maximum(diff_f, 0.0)), 0.0)
            dmat_ref[1, h] = jnp.where(diff_b > 0, jnp.exp(lg_b * jnp.maximum(diff_b, 0.0)), 0.0)
            xi_ref[0, h] = jnp.exp(lq_f * (rowq + 1.0))
            xi_ref[1, h] = jnp.exp(lq_b * (ch - rowq))
            zeta_ref[0, h] = jnp.exp(lq_f * (ch - 1.0 - rowq))
            zeta_ref[1, h] = jnp.exp(lq_b * rowq)

    dirs = ((0, qf_ref, kf_ref, vf_ref, of_ref, df_ref),
            (1, qb_ref, kb_ref, vb_ref, ob_ref, db_ref))
    work = [(d, h) + tuple(refs) for h in range(N_HEADS) for (d, *refs) in dirs]
    qs = lambda h: slice(h * QK_DIM, (h + 1) * QK_DIM)
    vs = lambda h: slice(h * V_DIM, (h + 1) * V_DIM)

    def rows(d, sub):
        j = sub if d == 0 else RET_CHUNKS - 1 - sub
        return slice(j * ch, (j + 1) * ch)

    scores = {}
    for sub in range(RET_CHUNKS):
        for (d, h, q, k, v, o, dec) in work:
            r = rows(d, sub)
            sc = lax.dot_general(q[r, qs(h)], k[r, qs(h)], (((1,), (1,)), ((), ())),
                                 preferred_element_type=F32)
            scores[sub, d, h] = (sc * dmat_ref[d, h]).astype(BF16)

    for sub in range(RET_CHUNKS):
        for (d, h, q, k, v, o, dec) in work:
            r = rows(d, sub)
            qx = (q[r, qs(h)].astype(F32) * xi_ref[d, h]).astype(BF16)
            lhs = jnp.concatenate([scores[sub, d, h], qx], axis=1)
            rhs = jnp.concatenate([v[r, vs(h)], state_ref[d, h].astype(BF16)], axis=0)
            o[r, vs(h)] = jnp.dot(lhs, rhs, preferred_element_type=F32).astype(BF16)
        for (d, h, q, k, v, o, dec) in work:
            r = rows(d, sub)
            kz = (k[r, qs(h)].astype(F32) * zeta_ref[d, h]).astype(BF16)
            upd = lax.dot_general(kz, v[r, vs(h)], (((0,), (0,)), ((), ())),
                                  preferred_element_type=F32)
            cd = jnp.exp(_log_sigmoid(dec[h]) * float(ch))
            state_ref[d, h] = state_ref[d, h] * cd + upd


def _retention(qk_arr, vg, dec_f, dec_b, nseq, seq):
    t = vg.shape[0]
    ch = CHUNK * RET_CHUNKS
    n = seq // ch
    assert seq % ch == 0 and t == nseq * seq

    def fwd(colblk):
        return lambda b, c: (b * n + c, colblk)

    def bwd(colblk):
        return lambda b, c: (b * n + (n - 1 - c), colblk)

    qk = lambda f, blk: pl.BlockSpec((ch, D_QK), f(blk))
    vv = lambda f: pl.BlockSpec((ch, D_V), f(0))
    dec = pl.BlockSpec((N_HEADS, 1, V_DIM), lambda b, c: (0, 0, 0))
    q_blk, k_blk = 0, 1
    out = jax.ShapeDtypeStruct((t, D_V), BF16)
    return pl.pallas_call(
        _ret_kernel,
        grid=(nseq, n),
        in_specs=[qk(fwd, q_blk), qk(fwd, k_blk), vv(fwd),
                  qk(bwd, q_blk), qk(bwd, k_blk), vv(bwd), dec, dec],
        out_specs=[pl.BlockSpec((ch, D_V), fwd(0)), pl.BlockSpec((ch, D_V), bwd(0))],
        out_shape=[out, out],
        scratch_shapes=[pltpu.VMEM((2, N_HEADS, CHUNK, CHUNK), F32),
                        pltpu.VMEM((2, N_HEADS, CHUNK, QK_DIM), F32),
                        pltpu.VMEM((2, N_HEADS, CHUNK, QK_DIM), F32),
                        pltpu.VMEM((2, N_HEADS, QK_DIM, V_DIM), F32)],
        compiler_params=_params("arbitrary", "arbitrary"),
        name="retention",
    )(qk_arr, qk_arr, vg, qk_arr, qk_arr, vg, dec_f, dec_b)


def _mix_kernel(of_ref, ob_ref, g_ref, gc_ref, gr_ref, uc_ref, gn_ref, wr_ref, wc_ref,
                o_ref, rg_ref):
    yc = gc_ref[...].astype(F32) * jnp.dot(uc_ref[...], wc_ref[...], preferred_element_type=F32)
    for h in range(N_HEADS):
        vs = slice(h * V_DIM, (h + 1) * V_DIM)
        r = of_ref[:, vs].astype(F32) + ob_ref[:, vs].astype(F32)
        mu = jnp.mean(r, axis=-1, keepdims=True)
        d = r - mu
        var = jnp.mean(d * d, axis=-1, keepdims=True)
        rn = d * lax.rsqrt(var + EPS) * gn_ref[:, vs]
        rg_ref[:, vs] = (g_ref[:, vs].astype(F32) * rn).astype(BF16)
    y_ret = jnp.dot(rg_ref[...], wr_ref[...], preferred_element_type=F32)
    mixed = yc + gr_ref[...].astype(F32) * y_ret
    o_ref[...] = mixed.astype(BF16)


def _out_kernel(m_ref, x_ref, wo_ref, o_ref):
    o_ref[...] = x_ref[...] + jnp.dot(m_ref[...], wo_ref[...], preferred_element_type=F32)


def _merge(o_f, o_b, vg, gates, uc, x1, wts):
    t = x1.shape[0]
    tm = MIX_TM
    assert t % tm == 0 and t % OUT_TM == 0
    big = [wts["ret_gn_w"], wts["w_ret_o"], wts["w_conv_o"]]
    mixed = pl.pallas_call(
        _mix_kernel,
        grid=(t // tm,),
        in_specs=[_rows(tm, D_V), _rows(tm, D_V), _rows(tm, D_V, 1),
                  _rows(tm, D_MODEL, 0), _rows(tm, D_MODEL, 1),
                  _rows(tm, D_CONV)] + [_full(a) for a in big],
        out_specs=_rows(tm, D_MODEL),
        out_shape=jax.ShapeDtypeStruct((t, D_MODEL), BF16),
        scratch_shapes=[pltpu.VMEM((tm, D_V), BF16)],
        compiler_params=_params("parallel"),
        name="mix",
    )(o_f, o_b, vg, gates, gates, uc, *big)
    to = OUT_TM
    return pl.pallas_call(
        _out_kernel,
        grid=(t // to,),
        in_specs=[_rows(to, D_MODEL), _rows(to, D_MODEL), _full(wts["w_out"])],
        out_specs=_rows(to, D_MODEL),
        out_shape=jax.ShapeDtypeStruct((t, D_MODEL), F32),
        compiler_params=_params("parallel"),
        name="out_proj",
    )(mixed, x1, wts["w_out"])


def _rope_tables(seq):
    half = QK_DIM // 2
    assert seq % ROPE_BLOCK == 0
    inv = ROPE_BASE ** (-jnp.arange(half, dtype=F32) / half)
    coarse = (jnp.arange(seq // ROPE_BLOCK, dtype=F32) * ROPE_BLOCK)[:, None] * inv[None, :]
    fine = jnp.arange(ROPE_BLOCK, dtype=F32)[:, None] * inv[None, :]
    ca, sa = jnp.cos(coarse)[:, None, :], jnp.sin(coarse)[:, None, :]
    cb, sb = jnp.cos(fine)[None, :, :], jnp.sin(fine)[None, :, :]
    cos = (ca * cb - sa * sb).reshape(seq, half)
    sin = (sa * cb + ca * sb).reshape(seq, half)
    return (jnp.concatenate([cos, cos], axis=-1),
            jnp.concatenate([-sin, sin], axis=-1))


def _trunk(x, wts):
    nseq, seq, _ = x.shape
    xf = x.reshape(nseq * seq, D_MODEL)
    x1, h = _ffn(xf, wts["ffn1_norm"], wts["ffn1_w1"], wts["ffn1_w3"], wts["ffn1_w2"],
                 wts["mix_norm"], emit_normed=True, name="ffn1")
    u, qk, vg, gates = _in_proj(h, wts, wts["rope_cos"], wts["rope_sin"], seq)
    uc = _conv_branch(u, wts, seq)
    o_f, o_b = _retention(qk, vg, wts["dec_f"], wts["dec_b"], nseq, seq)
    x2 = _merge(o_f, o_b, vg, gates, uc, x1, wts)
    (y,) = _ffn(x2, wts["ffn2_norm"], wts["ffn2_w1"], wts["ffn2_w3"], wts["ffn2_w2"],
                wts["final_norm"], emit_normed=False, name="ffn2")
    return y.reshape(nseq, seq, D_MODEL)


def _cast_cols_kernel(w_ref, *o_refs):
    lo = 0
    for o_ref in o_refs:
        n = o_ref.shape[1]
        o_ref[...] = w_ref[:, lo:lo + n].astype(BF16)
        lo += n


def _cast_split_cols(w, splits, br):
    r, c = w.shape
    assert sum(splits) == c and r % br == 0 and all(n % LANES == 0 for n in splits)
    return pl.pallas_call(
        _cast_cols_kernel,
        grid=(r // br,),
        in_specs=[_rows(br, c)],
        out_specs=[_rows(br, n) for n in splits],
        out_shape=[jax.ShapeDtypeStruct((r, n), BF16) for n in splits],
        compiler_params=_params("parallel"),
        name="cast_cols",
    )(w)


def _cast_kernel(w_ref, o_ref):
    o_ref[...] = w_ref[...].astype(BF16)


def _cast_split_rows(w, splits, br):
    r, c = w.shape
    assert sum(splits) == r
    out, lo = [], 0
    for n in splits:
        assert n % br == 0 and lo % br == 0
        off = lo // br
        out.append(pl.pallas_call(
            _cast_kernel,
            grid=(n // br,),
            in_specs=[pl.BlockSpec((br, c), lambda i, off=off: (i + off, 0))],
            out_specs=_rows(br, c),
            out_shape=jax.ShapeDtypeStruct((n, c), BF16),
            compiler_params=_params("parallel"),
            name="cast_rows",
        )(w))
        lo += n
    return out


def kernel(x_prompt, x_sample, ffn1_norm, ffn1_w1, ffn1_w3, ffn1_w2, mix_norm, w_in, dw_w, dw_b,
           conv_ln_w, conv_ln_b, w_conv_o, decay_fwd, decay_bwd, ret_gn_w, w_ret_o, w_out,
           ffn2_norm, ffn2_w1, ffn2_w3, ffn2_w2, final_norm):
    assert ffn1_w1.shape[0] == 1, "single layer"
    bf = lambda a: a.astype(BF16)
    vec = lambda a: a.reshape(1, -1).astype(F32)
    ffn_cols = lambda w: _cast_split_cols(w, FFN_SPLITS, CAST_ROWS)
    ffn_rows = lambda w: _cast_split_rows(w, FFN_SPLITS, CAST_ROWS)
    w_ab, w_qk, w_vg, w_gates = _cast_split_cols(
        w_in[0], (2 * D_CONV, 2 * D_QK, 2 * D_V, 2 * D_MODEL), CAST_ROWS // 2)
    rope_cos, rope_sin = _rope_tables(max(x_prompt.shape[1], x_sample.shape[1]))
    dec = lambda a: jnp.broadcast_to(a[0].astype(F32)[:, None, None], (N_HEADS, 1, V_DIM))
    wts = dict(
        ffn1_norm=vec(ffn1_norm[0]), ffn1_w1=ffn_cols(ffn1_w1[0]),
        ffn1_w3=ffn_cols(ffn1_w3[0]), ffn1_w2=ffn_rows(ffn1_w2[0]),
        mix_norm=vec(mix_norm[0]),
        w_ab=w_ab, w_qk=w_qk, w_vg=w_vg, w_gates=w_gates,
        rope_cos=rope_cos, rope_sin=rope_sin,
        dw_w=dw_w[0].astype(F32), dw_b=vec(dw_b[0]),
        conv_ln_w=vec(conv_ln_w[0]), conv_ln_b=vec(conv_ln_b[0]),
        w_conv_o=bf(w_conv_o[0]), dec_f=dec(decay_fwd), dec_b=dec(decay_bwd),
        ret_gn_w=vec(ret_gn_w[0]), w_ret_o=bf(w_ret_o[0]), w_out=bf(w_out[0]),
        ffn2_norm=vec(ffn2_norm[0]), ffn2_w1=ffn_cols(ffn2_w1[0]),
        ffn2_w3=ffn_cols(ffn2_w3[0]), ffn2_w2=ffn_rows(ffn2_w2[0]),
        final_norm=vec(final_norm),
    )
    return (_trunk(x_prompt, wts), _trunk(x_sample, wts))
```

```python
import functools

import jax
import jax.numpy as jnp
from jax import lax
from jax.experimental import pallas as pl
from jax.experimental.pallas import tpu as pltpu

D_MODEL = 2048
D_CONV = 1024
CONV_WIDTH = 31
N_HEADS = 8
QK_DIM = 128
V_DIM = 256
D_QK = N_HEADS * QK_DIM
D_V = N_HEADS * V_DIM
CHUNK = 128
RET_CHUNKS = 4
D_FF = 5632
ROPE_BASE = 10000.0
ROPE_BLOCK = 128
EPS = 1e-6

BF16 = jnp.bfloat16
F32 = jnp.float32

VMEM_LIMIT_BYTES = 56 * 1024 * 1024
SUBLANES = 8
LANES = 128

FFN_SPLITS = (2304, 2304, 1024)
FFN_TM = 512
FFN_LAST_TM = 512
PROJ_TM = 1024
PROJ_TN = 1024
CONV_HALO = 16
CONV_ROWS = 64
MIX_TM = 512
CAST_ROWS = 256
OUT_TM = 1024

assert sum(FFN_SPLITS) == D_FF


def _params(*sem):
    return pltpu.CompilerParams(dimension_semantics=sem,
                                vmem_limit_bytes=VMEM_LIMIT_BYTES)


def _rms(x, g):
    ms = jnp.mean(x * x, axis=-1, keepdims=True)
    return x * lax.rsqrt(ms + EPS) * g


def _silu(x):
    return x * jax.nn.sigmoid(x)


def _full(a):
    return pl.BlockSpec(a.shape, lambda i: (0,) * a.ndim)


def _rows(tm, width, blk=0):
    return pl.BlockSpec((tm, width), lambda i: (i, blk))


def _swiglu_partial(hn, w1_ref, w3_ref, w2_ref):
    a = jnp.dot(hn, w1_ref[...], preferred_element_type=F32)
    b = jnp.dot(hn, w3_ref[...], preferred_element_type=F32)
    p = (_silu(a) * b).astype(BF16)
    return jnp.dot(p, w2_ref[...], preferred_element_type=F32)


def _row_halves(ref):
    half = ref.shape[0] // 2
    return (slice(0, half), slice(half, 2 * half))


def _ffn_first_kernel(x_ref, g_ref, w1_ref, w3_ref, w2_ref, acc_ref, hn_ref):
    for rs in _row_halves(x_ref):
        hn = _rms(x_ref[rs, :], g_ref[...]).astype(BF16)
        hn_ref[rs, :] = hn
        acc_ref[rs, :] = _swiglu_partial(hn, w1_ref, w3_ref, w2_ref)


def _ffn_mid_kernel(hn_ref, acc_in_ref, w1_ref, w3_ref, w2_ref, acc_ref):
    acc_ref[...] = acc_in_ref[...] + _swiglu_partial(hn_ref[...], w1_ref, w3_ref, w2_ref)


def _ffn_last_kernel(hn_ref, acc_in_ref, x_ref, g2_ref, w1_ref, w3_ref, w2_ref,
                     y_ref, *h_ref, emit_normed):
    for rs in _row_halves(hn_ref):
        s = acc_in_ref[rs, :] + _swiglu_partial(hn_ref[rs, :], w1_ref, w3_ref, w2_ref)
        y = x_ref[rs, :] + 0.5 * s
        if emit_normed:
            y_ref[rs, :] = y
            h_ref[0][rs, :] = _rms(y, g2_ref[...]).astype(BF16)
        else:
            y_ref[rs, :] = _rms(y, g2_ref[...])


def _ffn(x, g, w1s, w3s, w2s, g2, *, emit_normed, name):
    t = x.shape[0]
    tm = FFN_TM
    assert t % tm == 0 and t % FFN_LAST_TM == 0 and len(w1s) >= 2
    f32_out = jax.ShapeDtypeStruct((t, D_MODEL), F32)
    bf_out = jax.ShapeDtypeStruct((t, D_MODEL), BF16)
    row = _rows(tm, D_MODEL)

    acc, hn = pl.pallas_call(
        _ffn_first_kernel,
        grid=(t // tm,),
        in_specs=[row, _full(g), _full(w1s[0]), _full(w3s[0]), _full(w2s[0])],
        out_specs=[row, row],
        out_shape=[f32_out, bf_out],
        compiler_params=_params("parallel"),
        name=name + "_first",
    )(x, g, w1s[0], w3s[0], w2s[0])

    for w1, w3, w2 in zip(w1s[1:-1], w3s[1:-1], w2s[1:-1]):
        acc = pl.pallas_call(
            _ffn_mid_kernel,
            grid=(t // tm,),
            in_specs=[row, row, _full(w1), _full(w3), _full(w2)],
            out_specs=row,
            out_shape=f32_out,
            compiler_params=_params("parallel"),
            name=name + "_mid",
        )(hn, acc, w1, w3, w2)

    tl = FFN_LAST_TM
    rowl = _rows(tl, D_MODEL)
    return pl.pallas_call(
        functools.partial(_ffn_last_kernel, emit_normed=emit_normed),
        grid=(t // tl,),
        in_specs=[rowl, rowl, rowl, _full(g2), _full(w1s[-1]), _full(w3s[-1]), _full(w2s[-1])],
        out_specs=[rowl, rowl] if emit_normed else [rowl],
        out_shape=[f32_out, bf_out] if emit_normed else [f32_out],
        compiler_params=_params("parallel"),
        name=name + "_last",
    )(hn, acc, x, g2, w1s[-1], w3s[-1], w2s[-1])


def _hdot(h, w_ref, lo, n):
    return jnp.dot(h, w_ref[:, lo:lo + n], preferred_element_type=F32)


def _convqk_proj_kernel(h_ref, wab_ref, wqk_ref, cos_ref, sin_ref, u_ref, qk_ref):
    h = h_ref[...]
    u_ref[...] = _hdot(h, wab_ref, 0, D_CONV) * jax.nn.sigmoid(_hdot(h, wab_ref, D_CONV, D_CONV))
    cos = cos_ref[...]
    sin = sin_ref[...]
    for part, scale in ((0, 1.0), (1, QK_DIM ** -0.5)):
        y = _hdot(h, wqk_ref, part * D_QK, D_QK)
        for hd in range(N_HEADS):
            sl = slice(hd * QK_DIM, (hd + 1) * QK_DIM)
            xh = y[:, sl]
            rot = xh * cos + pltpu.roll(xh, QK_DIM // 2, 1) * sin
            qk_ref[:, part * D_QK + hd * QK_DIM:part * D_QK + (hd + 1) * QK_DIM] = (
                rot * scale).astype(BF16)


def _vg_proj_kernel(h_ref, w_ref, o_ref):
    h = h_ref[...]
    for lo in range(0, 2 * D_V, PROJ_TN):
        y = _hdot(h, w_ref, lo, PROJ_TN)
        o_ref[:, lo:lo + PROJ_TN] = (y if lo < D_V else _silu(y)).astype(BF16)


def _gate_proj_kernel(h_ref, w_ref, o_ref):
    h = h_ref[...]
    for lo in range(0, 2 * D_MODEL, PROJ_TN):
        o_ref[:, lo:lo + PROJ_TN] = jax.nn.sigmoid(_hdot(h, w_ref, lo, PROJ_TN)).astype(BF16)


def _in_proj(h, wts, cosf, sinf, seq):
    t = h.shape[0]
    tm = PROJ_TM
    assert t % tm == 0 and seq % tm == 0
    tiles_per_seq = seq // tm
    hrow = _rows(tm, D_MODEL)
    tab = pl.BlockSpec((tm, QK_DIM), lambda i: (i % tiles_per_seq, 0))
    u, qk = pl.pallas_call(
        _convqk_proj_kernel,
        grid=(t // tm,),
        in_specs=[hrow, _full(wts["w_ab"]), _full(wts["w_qk"]), tab, tab],
        out_specs=[_rows(tm, D_CONV), _rows(tm, 2 * D_QK)],
        out_shape=[jax.ShapeDtypeStruct((t, D_CONV), F32),
                   jax.ShapeDtypeStruct((t, 2 * D_QK), BF16)],
        compiler_params=_params("parallel"),
        name="convqk_proj",
    )(h, wts["w_ab"], wts["w_qk"], cosf, sinf)
    wide = []
    for fn, w, name in ((_vg_proj_kernel, wts["w_vg"], "vg_proj"),
                        (_gate_proj_kernel, wts["w_gates"], "gate_proj")):
        wide.append(pl.pallas_call(
            fn,
            grid=(t // tm,),
            in_specs=[hrow, _full(w)],
            out_specs=_rows(tm, w.shape[1]),
            out_shape=jax.ShapeDtypeStruct((t, w.shape[1]), BF16),
            compiler_params=_params("parallel"),
            name=name,
        )(h, w))
    return u, qk, wide[0], wide[1]


def _dwconv_ln_silu(prev_ref, cur_ref, next_ref, w_ref, b_ref, lnw_ref, lnb_ref,
                    pad_ref, y_ref, il, tiles_per_seq):
    ts = cur_ref.shape[0]
    pad_ref[0:CONV_HALO, :] = jnp.where(il > 0, prev_ref[...], 0.0)
    pad_ref[CONV_HALO:CONV_HALO + ts, :] = cur_ref[...]
    pad_ref[CONV_HALO + ts:, :] = jnp.where(il < tiles_per_seq - 1, next_ref[...], 0.0)

    rows = CONV_ROWS
    for r0 in range(0, ts, rows):
        for c0 in range(0, D_CONV, LANES):
            cs = slice(c0, c0 + LANES)
            acc = None
            for b in range(SUBLANES):
                part = None
                for a in range(CONV_WIDTH // SUBLANES + 1):
                    s = SUBLANES * a + b
                    if s < 1 or s > CONV_WIDTH:
                        continue
                    xs = pad_ref[r0 + SUBLANES * a:r0 + SUBLANES * a + rows + SUBLANES, cs]
                    term = xs * w_ref[s - 1:s, cs]
                    part = term if part is None else part + term
                part = part[b:b + rows, :]
                acc = part if acc is None else acc + part
            y_ref[r0:r0 + rows, cs] = acc

    y = y_ref[...] + b_ref[...]
    mu = jnp.mean(y, axis=-1, keepdims=True)
    d = y - mu
    var = jnp.mean(d * d, axis=-1, keepdims=True)
    z = d * lax.rsqrt(var + EPS) * lnw_ref[...] + lnb_ref[...]
    return _silu(z).astype(BF16)


def _log_sigmoid(x):
    return -(jnp.maximum(-x, 0.0) + jnp.log1p(jnp.exp(-jnp.abs(x))))


def _ret_kernel(qf_ref, kf_ref, vf_ref, qb_ref, kb_ref, vb_ref, df_ref, db_ref,
                of_ref, ob_ref, dmat_ref, xi_ref, zeta_ref, state_ref):
    c = pl.program_id(1)
    ch = CHUNK

    @pl.when(c == 0)
    def _():
        state_ref[...] = jnp.zeros_like(state_ref)
        row = lax.broadcasted_iota(jnp.int32, (ch, ch), 0).astype(F32)
        col = lax.broadcasted_iota(jnp.int32, (ch, ch), 1).astype(F32)
        rowq = lax.broadcasted_iota(jnp.int32, (ch, QK_DIM), 0).astype(F32)
        for h in range(N_HEADS):
            lg_f = _log_sigmoid(df_ref[h])[:, :ch]
            lg_b = _log_sigmoid(db_ref[h])[:, :ch]
            lq_f = _log_sigmoid(df_ref[h])[:, :QK_DIM]
            lq_b = _log_sigmoid(db_ref[h])[:, :QK_DIM]
            diff_f = row - col
            diff_b = col - row
            dmat_ref[0, h] = jnp.where(diff_f >= 0, jnp.exp(lg_f * jnp.maximum(diff_f, 0.0)), 0.0)
            dmat_ref[1, h] = jnp.where(diff_b > 0, jnp.exp(lg_b * jnp.maximum(diff_b, 0.0)), 0.0)
            xi_ref[0, h] = jnp.exp(lq_f * (rowq + 1.0))
            xi_ref[1, h] = jnp.exp(lq_b * (ch - rowq))
            zeta_ref[0, h] = jnp.exp(lq_f * (ch - 1.0 - rowq))
            zeta_ref[1, h] = jnp.exp(lq_b * rowq)

    dirs = ((0, qf_ref, kf_ref, vf_ref, of_ref, df_ref),
            (1, qb_ref, kb_ref, vb_ref, ob_ref, db_ref))
    work = [(d, h) + tuple(refs) for h in range(N_HEADS) for (d, *refs) in dirs]
    qs = lambda h: slice(h * QK_DIM, (h + 1) * QK_DIM)
    vs = lambda h: slice(h * V_DIM, (h + 1) * V_DIM)

    def rows(d, sub):
        j = sub if d == 0 else RET_CHUNKS - 1 - sub
        return slice(j * ch, (j + 1) * ch)

    scores = {}
    for sub in range(RET_CHUNKS):
        for (d, h, q, k, v, o, dec) in work:
            r = rows(d, sub)
            sc = lax.dot_general(q[r, qs(h)], k[r, qs(h)], (((1,), (1,)), ((), ())),
                                 preferred_element_type=F32)
            scores[sub, d, h] = (sc * dmat_ref[d, h]).astype(BF16)

    for sub in range(RET_CHUNKS):
        for (d, h, q, k, v, o, dec) in work:
            r = rows(d, sub)
            qx = (q[r, qs(h)].astype(F32) * xi_ref[d, h]).astype(BF16)
            lhs = jnp.concatenate([scores[sub, d, h], qx], axis=1)
            rhs = jnp.concatenate([v[r, vs(h)], state_ref[d, h].astype(BF16)], axis=0)
            o[r, vs(h)] = jnp.dot(lhs, rhs, preferred_element_type=F32).astype(BF16)
        for (d, h, q, k, v, o, dec) in work:
            r = rows(d, sub)
            kz = (k[r, qs(h)].astype(F32) * zeta_ref[d, h]).astype(BF16)
            upd = lax.dot_general(kz, v[r, vs(h)], (((0,), (0,)), ((), ())),
                                  preferred_element_type=F32)
            cd = jnp.exp(_log_sigmoid(dec[h]) * float(ch))
            state_ref[d, h] = state_ref[d, h] * cd + upd


def _retention(qk_arr, vg, dec_f, dec_b, nseq, seq):
    t = vg.shape[0]
    ch = CHUNK * RET_CHUNKS
    n = seq // ch
    assert seq % ch == 0 and t == nseq * seq

    def fwd(colblk):
        return lambda b, c: (b * n + c, colblk)

    def bwd(colblk):
        return lambda b, c: (b * n + (n - 1 - c), colblk)

    qk = lambda f, blk: pl.BlockSpec((ch, D_QK), f(blk))
    vv = lambda f: pl.BlockSpec((ch, D_V), f(0))
    dec = pl.BlockSpec((N_HEADS, 1, V_DIM), lambda b, c: (0, 0, 0))
    q_blk, k_blk = 0, 1
    out = jax.ShapeDtypeStruct((t, D_V), BF16)
    return pl.pallas_call(
        _ret_kernel,
        grid=(nseq, n),
        in_specs=[qk(fwd, q_blk), qk(fwd, k_blk), vv(fwd),
                  qk(bwd, q_blk), qk(bwd, k_blk), vv(bwd), dec, dec],
        out_specs=[pl.BlockSpec((ch, D_V), fwd(0)), pl.BlockSpec((ch, D_V), bwd(0))],
        out_shape=[out, out],
        scratch_shapes=[pltpu.VMEM((2, N_HEADS, CHUNK, CHUNK), F32),
                        pltpu.VMEM((2, N_HEADS, CHUNK, QK_DIM), F32),
                        pltpu.VMEM((2, N_HEADS, CHUNK, QK_DIM), F32),
                        pltpu.VMEM((2, N_HEADS, QK_DIM, V_DIM), F32)],
        compiler_params=_params("arbitrary", "arbitrary"),
        name="retention",
    )(qk_arr, qk_arr, vg, qk_arr, qk_arr, vg, dec_f, dec_b)


def _mix_kernel(prev_ref, cur_ref, next_ref, dww_ref, dwb_ref, lnw_ref, lnb_ref,
                of_ref, ob_ref, g_ref, gc_ref, gr_ref, gn_ref, wr_ref, wc_ref,
                o_ref, rg_ref, pad_ref, y_ref, *, tiles_per_seq):
    for h in range(N_HEADS):
        vs = slice(h * V_DIM, (h + 1) * V_DIM)
        r = of_ref[:, vs].astype(F32) + ob_ref[:, vs].astype(F32)
        mu = jnp.mean(r, axis=-1, keepdims=True)
        d = r - mu
        var = jnp.mean(d * d, axis=-1, keepdims=True)
        rn = d * lax.rsqrt(var + EPS) * gn_ref[:, vs]
        rg_ref[:, vs] = (g_ref[:, vs].astype(F32) * rn).astype(BF16)
    yr = gr_ref[...].astype(F32) * jnp.dot(rg_ref[...], wr_ref[...], preferred_element_type=F32)
    il = pl.program_id(0) % tiles_per_seq
    uc = _dwconv_ln_silu(prev_ref, cur_ref, next_ref, dww_ref, dwb_ref, lnw_ref, lnb_ref,
                         pad_ref, y_ref, il, tiles_per_seq)
    y_conv = jnp.dot(uc, wc_ref[...], preferred_element_type=F32)
    o_ref[...] = (yr + gc_ref[...].astype(F32) * y_conv).astype(BF16)


def _out_kernel(m_ref, x_ref, wo_ref, o_ref):
    o_ref[...] = x_ref[...] + jnp.dot(m_ref[...], wo_ref[...], preferred_element_type=F32)


def _merge(o_f, o_b, vg, gates, u, x1, wts, seq):
    t = x1.shape[0]
    tm = MIX_TM
    assert t % tm == 0 and t % OUT_TM == 0 and seq % tm == 0 and tm % CONV_ROWS == 0
    assert tm % CONV_HALO == 0 and CONV_HALO > CONV_WIDTH // 2
    hb = tm // CONV_HALO
    last = t // CONV_HALO - 1
    small = [wts["dw_w"], wts["dw_b"], wts["conv_ln_w"], wts["conv_ln_b"]]
    big = [wts["ret_gn_w"], wts["w_ret_o"], wts["w_conv_o"]]
    mixed = pl.pallas_call(
        functools.partial(_mix_kernel, tiles_per_seq=seq // tm),
        grid=(t // tm,),
        in_specs=[pl.BlockSpec((CONV_HALO, D_CONV), lambda i: (jnp.maximum(i * hb - 1, 0), 0)),
                  _rows(tm, D_CONV),
                  pl.BlockSpec((CONV_HALO, D_CONV), lambda i: (jnp.minimum((i + 1) * hb, last), 0))]
                 + [_full(a) for a in small]
                 + [_rows(tm, D_V), _rows(tm, D_V), _rows(tm, D_V, 1),
                    _rows(tm, D_MODEL, 0), _rows(tm, D_MODEL, 1)] + [_full(a) for a in big],
        out_specs=_rows(tm, D_MODEL),
        out_shape=jax.ShapeDtypeStruct((t, D_MODEL), BF16),
        scratch_shapes=[pltpu.VMEM((tm, D_V), BF16),
                        pltpu.VMEM((tm + 2 * CONV_HALO, D_CONV), F32),
                        pltpu.VMEM((tm, D_CONV), F32)],
        compiler_params=_params("parallel"),
        name="mix",
    )(u, u, u, *small, o_f, o_b, vg, gates, gates, *big)
    to = OUT_TM
    return pl.pallas_call(
        _out_kernel,
        grid=(t // to,),
        in_specs=[_rows(to, D_MODEL), _rows(to, D_MODEL), _full(wts["w_out"])],
        out_specs=_rows(to, D_MODEL),
        out_shape=jax.ShapeDtypeStruct((t, D_MODEL), F32),
        compiler_params=_params("parallel"),
        name="out_proj",
    )(mixed, x1, wts["w_out"])


def _rope_tables(seq):
    half = QK_DIM // 2
    assert seq % ROPE_BLOCK == 0
    inv = ROPE_BASE ** (-jnp.arange(half, dtype=F32) / half)
    coarse = (jnp.arange(seq // ROPE_BLOCK, dtype=F32) * ROPE_BLOCK)[:, None] * inv[None, :]
    fine = jnp.arange(ROPE_BLOCK, dtype=F32)[:, None] * inv[None, :]
    ca, sa = jnp.cos(coarse)[:, None, :], jnp.sin(coarse)[:, None, :]
    cb, sb = jnp.cos(fine)[None, :, :], jnp.sin(fine)[None, :, :]
    cos = (ca * cb - sa * sb).reshape(seq, half)
    sin = (sa * cb + ca * sb).reshape(seq, half)
    return (jnp.concatenate([cos, cos], axis=-1),
            jnp.concatenate([-sin, sin], axis=-1))


def _trunk(x, wts):
    nseq, seq, _ = x.shape
    xf = x.reshape(nseq * seq, D_MODEL)
    x1, h = _ffn(xf, wts["ffn1_norm"], wts["ffn1_w1"], wts["ffn1_w3"], wts["ffn1_w2"],
                 wts["mix_norm"], emit_normed=True, name="ffn1")
    u, qk, vg, gates = _in_proj(h, wts, wts["rope_cos"], wts["rope_sin"], seq)
    o_f, o_b = _retention(qk, vg, wts["dec_f"], wts["dec_b"], nseq, seq)
    x2 = _merge(o_f, o_b, vg, gates, u, x1, wts, seq)
    (y,) = _ffn(x2, wts["ffn2_norm"], wts["ffn2_w1"], wts["ffn2_w3"], wts["ffn2_w2"],
                wts["final_norm"], emit_normed=False, name="ffn2")
    return y.reshape(nseq, seq, D_MODEL)


def _cast_cols_kernel(w_ref, *o_refs):
    lo = 0
    for o_ref in o_refs:
        n = o_ref.shape[1]
        o_ref[...] = w_ref[:, lo:lo + n].astype(BF16)
        lo += n


def _cast_split_cols(w, splits, br):
    r, c = w.shape
    assert sum(splits) == c and r % br == 0 and all(n % LANES == 0 for n in splits)
    return pl.pallas_call(
        _cast_cols_kernel,
        grid=(r // br,),
        in_specs=[_rows(br, c)],
        out_specs=[_rows(br, n) for n in splits],
        out_shape=[jax.ShapeDtypeStruct((r, n), BF16) for n in splits],
        compiler_params=_params("parallel"),
        name="cast_cols",
    )(w)


def _cast_kernel(w_ref, o_ref):
    o_ref[...] = w_ref[...].astype(BF16)


def _cast_split_rows(w, splits, br):
    r, c = w.shape
    assert sum(splits) == r
    out, lo = [], 0
    for n in splits:
        assert n % br == 0 and lo % br == 0
        off = lo // br
        out.append(pl.pallas_call(
            _cast_kernel,
            grid=(n // br,),
            in_specs=[pl.BlockSpec((br, c), lambda i, off=off: (i + off, 0))],
            out_specs=_rows(br, c),
            out_shape=jax.ShapeDtypeStruct((n, c), BF16),
            compiler_params=_params("parallel"),
            name="cast_rows",
        )(w))
        lo += n
    return out


def kernel(x_prompt, x_sample, ffn1_norm, ffn1_w1, ffn1_w3, ffn1_w2, mix_norm, w_in, dw_w, dw_b,
           conv_ln_w, conv_ln_b, w_conv_o, decay_fwd, decay_bwd, ret_gn_w, w_ret_o, w_out,
           ffn2_norm, ffn2_w1, ffn2_w3, ffn2_w2, final_norm):
    assert ffn1_w1.shape[0] == 1, "single layer"
    bf = lambda a: a.astype(BF16)
    vec = lambda a: a.reshape(1, -1).astype(F32)
    ffn_cols = lambda w: _cast_split_cols(w, FFN_SPLITS, CAST_ROWS)
    ffn_rows = lambda w: _cast_split_rows(w, FFN_SPLITS, CAST_ROWS)
    w_ab, w_qk, w_vg, w_gates = _cast_split_cols(
        w_in[0], (2 * D_CONV, 2 * D_QK, 2 * D_V, 2 * D_MODEL), CAST_ROWS // 2)
    rope_cos, rope_sin = _rope_tables(max(x_prompt.shape[1], x_sample.shape[1]))
    dec = lambda a: jnp.broadcast_to(a[0].astype(F32)[:, None, None], (N_HEADS, 1, V_DIM))
    wts = dict(
        ffn1_norm=vec(ffn1_norm[0]), ffn1_w1=ffn_cols(ffn1_w1[0]),
        ffn1_w3=ffn_cols(ffn1_w3[0]), ffn1_w2=ffn_rows(ffn1_w2[0]),
        mix_norm=vec(mix_norm[0]),
        w_ab=w_ab, w_qk=w_qk, w_vg=w_vg, w_gates=w_gates,
        rope_cos=rope_cos, rope_sin=rope_sin,
        dw_w=dw_w[0].astype(F32), dw_b=vec(dw_b[0]),
        conv_ln_w=vec(conv_ln_w[0]), conv_ln_b=vec(conv_ln_b[0]),
        w_conv_o=bf(w_conv_o[0]), dec_f=dec(decay_fwd), dec_b=dec(decay_bwd),
        ret_gn_w=vec(ret_gn_w[0]), w_ret_o=bf(w_ret_o[0]), w_out=bf(w_out[0]),
        ffn2_norm=vec(ffn2_norm[0]), ffn2_w1=ffn_cols(ffn2_w1[0]),
        ffn2_w3=ffn_cols(ffn2_w3[0]), ffn2_w2=ffn_rows(ffn2_w2[0]),
        final_norm=vec(final_norm),
    )
    return (_trunk(x_prompt, wts), _trunk(x_sample, wts))
```
